```python
import jax, jax.numpy as jnp
from jax import lax
import numpy as np

D_MODEL = 1024
BATCH = 4
SEQ = 4096
DEPTH = 2
DEC_BATCH = 32
DEC_SEQ = 64
PAST_LEN = 2048

CHUNK = 64
EPS = 1e-6
MLA_HEADS = 4
Q_LORA = 384
KV_LORA = 256
QK_NOPE = 128
QK_ROPE = 64
V_HEAD = 128
MLA_WIDTH = MLA_HEADS * V_HEAD
MLA_SCALE = (QK_NOPE + QK_ROPE) ** -0.5
ROPE_BASE = 10000.0
Q_BLOCK = 128
HGRN_HEADS = 4
HGRN_EXPAND = 64
HGRN_HEAD_DIM = 64
HGRN_KEY = HGRN_HEADS * HGRN_EXPAND
HGRN_WIDTH = HGRN_HEADS * HGRN_HEAD_DIM
CMLP_GROUPS = 4
CMLP_GROUP_DIM = 64
CMLP_WIDTH = CMLP_GROUPS * CMLP_GROUP_DIM
CMLP_CHUNK = 128
MIX_WIDTH = MLA_WIDTH + HGRN_WIDTH + CMLP_WIDTH
D_FF = 2816
IN_SIZES = (Q_LORA, KV_LORA, QK_ROPE, HGRN_KEY, HGRN_KEY, HGRN_WIDTH, HGRN_WIDTH, CMLP_WIDTH, CMLP_WIDTH)
IN_WIDTH = Q_LORA + KV_LORA + QK_ROPE + 2 * HGRN_KEY + 2 * HGRN_WIDTH + 2 * CMLP_WIDTH

kernel_name = 'hybrid_stream_mla_hgrn2_gmlp_step'


def _in_offsets():
    out, acc = [], 0
    for s in IN_SIZES[:-1]:
        acc += s
        out.append(acc)
    return out


def _rmsnorm(x, g):
    x32 = x.astype(jnp.float32)
    y = x32 * lax.rsqrt(jnp.mean(x32 * x32, axis=-1, keepdims=True) + EPS)
    return (y * g.astype(jnp.float32)).astype(x.dtype)


def _swiglu(x, w_gate, w_up, w_down):
    return (jax.nn.silu(x @ w_gate) * (x @ w_up)) @ w_down


def _rope_tables(pos, dim):
    inv_freq = ROPE_BASE ** (-jnp.arange(0, dim, 2, dtype=jnp.float32) / dim)
    ang = pos.astype(jnp.float32)[:, None] * inv_freq[None, :]
    return jnp.cos(ang), jnp.sin(ang)


def _apply_rope(x, cos, sin):
    x32 = x.astype(jnp.float32)
    half = x.shape[-1] // 2
    x1, x2 = x32[..., :half], x32[..., half:]
    return jnp.concatenate([x1 * cos - x2 * sin, x2 * cos + x1 * sin], axis=-1).astype(x.dtype)


def _mla_attend(q_lat, q_pe, c_all, kpe_all, q_pos, k_pos):
    s = (jnp.einsum('bqhc,bkc->bhqk', q_lat, c_all)
         + jnp.einsum('bqhr,bkr->bhqk', q_pe, kpe_all)).astype(jnp.float32) * MLA_SCALE
    visible = (k_pos[None, :] // CHUNK) <= (q_pos[:, None] // CHUNK)
    s = jnp.where(visible[None, None], s, -jnp.inf)
    p = jax.nn.softmax(s, axis=-1).astype(c_all.dtype)
    return jnp.einsum('bhqk,bkc->bqhc', p, c_all)


def _mla(q_a, kv_a, kpe_raw, pos, qa_g, w_qb, kva_g, w_uk, w_uv, c_past, kpe_past):
    B, T, _ = q_a.shape
    q = (_rmsnorm(q_a, qa_g) @ w_qb).reshape(B, T, MLA_HEADS, QK_NOPE + QK_ROPE)
    q_nope, q_pe = q[..., :QK_NOPE], q[..., QK_NOPE:]
    cos, sin = _rope_tables(pos, QK_ROPE)
    q_pe = _apply_rope(q_pe, cos[:, None, :], sin[:, None, :])
    c_new = _rmsnorm(kv_a, kva_g)
    kpe_new = _apply_rope(kpe_raw, cos, sin)
    q_lat = jnp.einsum('bthn,chn->bthc', q_nope, w_uk)
    if c_past is None:
        c_all, kpe_all = c_new, kpe_new
    else:
        c_all = jnp.concatenate([c_past.astype(c_new.dtype), c_new], axis=1)
        kpe_all = jnp.concatenate([kpe_past.astype(kpe_new.dtype), kpe_new], axis=1)
    k_pos = jnp.arange(c_all.shape[1], dtype=jnp.int32)
    if T > Q_BLOCK and T % Q_BLOCK == 0:
        nb = T // Q_BLOCK
        blk = lambda a: jnp.moveaxis(a.reshape((B, nb, Q_BLOCK) + a.shape[2:]), 1, 0)
        o_lat = lax.map(lambda args: _mla_attend(args[0], args[1], c_all, kpe_all, args[2], k_pos),
                        (blk(q_lat), blk(q_pe), pos.reshape(nb, Q_BLOCK)))
        o_lat = jnp.moveaxis(o_lat, 0, 1).reshape(B, T, MLA_HEADS, KV_LORA)
    else:
        o_lat = _mla_attend(q_lat, q_pe, c_all, kpe_all, pos, k_pos)
    o = jnp.einsum('bthc,chv->bthv', o_lat, w_uv).reshape(B, T, MLA_WIDTH)
    return o, c_new, kpe_new


def _hgrn2(hq, hf, hi, hg, lb, out_g, S0):
    B, T, _ = hq.shape
    f32 = jnp.float32
    zf = hf.astype(f32)
    logf = jnp.logaddexp(jnp.log(lb), jnp.log1p(-lb) + jax.nn.log_sigmoid(zf))
    k = (1.0 - lb) * jax.nn.sigmoid(-zf)
    q = hq.astype(f32).reshape(B, T, HGRN_HEADS, HGRN_EXPAND)
    k = k.reshape(B, T, HGRN_HEADS, HGRN_EXPAND)
    logf = logf.reshape(B, T, HGRN_HEADS, HGRN_EXPAND)
    v = hi.astype(f32).reshape(B, T, HGRN_HEADS, HGRN_HEAD_DIM)
    L = min(CHUNK, T)
    n = T // L
    to_chunks = lambda a: jnp.moveaxis(a.reshape((B, n, L) + a.shape[2:]), 1, 0)
    causal = jnp.tril(jnp.ones((L, L), dtype=bool))[None, :, :, None, None]

    def step(S, inp):
        qc, kc, vc, lc = inp
        b = jnp.cumsum(lc, axis=1)
        o_inter = jnp.einsum('bthe,bhed->bthd', qc * jnp.exp(b), S)
        diff = b[:, :, None] - b[:, None, :]
        decay = jnp.where(causal, jnp.exp(jnp.where(causal, diff, 0.0)), 0.0)
        A = jnp.einsum('bthe,btshe,bshe->bhts', qc, decay, kc)
        o_intra = jnp.einsum('bhts,bshd->bthd', A, vc)
        b_last = b[:, -1]
        S_new = (jnp.exp(b_last)[..., None] * S
                 + jnp.einsum('bshe,bshd->bhed', kc * jnp.exp(b_last[:, None] - b), vc))
        return S_new, o_inter + o_intra

    if S0 is None:
        S0 = jnp.zeros((B, HGRN_HEADS, HGRN_EXPAND, HGRN_HEAD_DIM), f32)
    else:
        S0 = S0.astype(f32)
    S_fin, o = lax.scan(step, S0, (to_chunks(q), to_chunks(k), to_chunks(v), to_chunks(logf)))
    o = jnp.moveaxis(o, 0, 1).reshape(B, T, HGRN_HEADS, HGRN_HEAD_DIM)
    gate = jax.nn.silu(hg.astype(f32).reshape(B, T, HGRN_HEADS, HGRN_HEAD_DIM))
    o = _rmsnorm(o, out_g.reshape(HGRN_HEADS, HGRN_HEAD_DIM)) * gate
    return o.reshape(B, T, HGRN_WIDTH).astype(hq.dtype), S_fin


def _chunk_mlp(cu, cv, v_g, w_s, b_s):
    B, T, _ = cu.shape
    u = jax.nn.gelu(cu)
    v = _rmsnorm(jax.nn.gelu(cv).reshape(B, T, CMLP_GROUPS, CMLP_GROUP_DIM),
                 v_g.reshape(CMLP_GROUPS, CMLP_GROUP_DIM))
    L = min(CMLP_CHUNK, T)
    n = T // L
    w = w_s[:, :L, :L] * jnp.tril(jnp.ones((L, L), w_s.dtype))
    mixed = (jnp.einsum('gts,bnsgd->bntgd', w, v.reshape(B, n, L, CMLP_GROUPS, CMLP_GROUP_DIM))
             + jnp.transpose(b_s[:, :L])[None, None, :, :, None])
    out = u * mixed.reshape(B, T, CMLP_WIDTH).astype(u.dtype)
    return out, v.reshape(B, T, CMLP_WIDTH)


def _trunk(x, pos, cache_ckv, cache_kpe, state_S, p):
    offsets = _in_offsets()
    lb_all = jnp.cumsum(jax.nn.softmax(p['hgrn_lb_logits'].astype(jnp.float32), axis=0), axis=0)
    lb_all = lb_all - lb_all[0:1]
    ckv_rows, kpe_rows, states, v_rows = [], [], [], []
    for l in range(DEPTH):
        ng = p['norm_g'][l]
        h = _rmsnorm(x, ng[0])
        x = x + 0.5 * _rmsnorm(_swiglu(h, p['ffn_w_gate'][l, 0], p['ffn_w_up'][l, 0], p['ffn_w_down'][l, 0]), ng[1])
        h = _rmsnorm(x, ng[2])
        z = h @ p['w_in'][l]
        q_a, kv_a, kpe_raw, hq, hf, hi, hg, cu, cv = jnp.split(z, offsets, axis=-1)
        o_a, c_new, kpe_new = _mla(q_a, kv_a, kpe_raw, pos, p['mla_qa_g'][l], p['mla_wqb'][l],
                                   p['mla_kva_g'][l], p['mla_w_uk'][l], p['mla_w_uv'][l],
                                   None if cache_ckv is None else cache_ckv[l],
                                   None if cache_kpe is None else cache_kpe[l])
        o_b, S_new = _hgrn2(hq, hf, hi, hg, lb_all[l], p['hgrn_out_g'][l],
                            None if state_S is None else state_S[l])
        o_c, v_new = _chunk_mlp(cu, cv, p['cmlp_v_g'][l], p['cmlp_w_s'][l], p['cmlp_b_s'][l])
        mix = jnp.concatenate([o_a, o_b.astype(o_a.dtype), o_c.astype(o_a.dtype)], axis=-1) @ p['w_out'][l]
        x = x + _rmsnorm(mix, ng[3])
        h = _rmsnorm(x, ng[4])
        x = x + 0.5 * _rmsnorm(_swiglu(h, p['ffn_w_gate'][l, 1], p['ffn_w_up'][l, 1], p['ffn_w_down'][l, 1]), ng[5])
        ckv_rows.append(c_new)
        kpe_rows.append(kpe_new)
        states.append(S_new)
        v_rows.append(v_new)
    return x, jnp.stack(ckv_rows), jnp.stack(kpe_rows), jnp.stack(states), jnp.stack(v_rows)


def setup_inputs(seed: int = 0) -> dict:
    key = jax.random.key(seed)
    ks = jax.random.split(key, 24)
    f32 = jnp.float32
    nrm = lambda k, shape, scale: jax.random.normal(k, shape, f32) * scale
    gain = lambda k, shape: 1.0 + 0.02 * jax.random.normal(k, shape, f32)
    return {
        'x_prompt': nrm(ks[0], (BATCH, SEQ, D_MODEL), 1.0),
        'x_sample': nrm(ks[1], (DEC_BATCH, DEC_SEQ, D_MODEL), 1.0),
        'cache_mla_ckv': nrm(ks[2], (DEPTH, DEC_BATCH, PAST_LEN, KV_LORA), 1.0),
        'cache_mla_kpe': nrm(ks[3], (DEPTH, DEC_BATCH, PAST_LEN, QK_ROPE), 1.0),
        'state_hgrn': nrm(ks[4], (DEPTH, DEC_BATCH, HGRN_HEADS, HGRN_EXPAND, HGRN_HEAD_DIM), 0.5),
        'norm_g': gain(ks[5], (DEPTH, 6, D_MODEL)),
        'ffn_w_gate': nrm(ks[6], (DEPTH, 2, D_MODEL, D_FF), D_MODEL ** -0.5),
        'ffn_w_up': nrm(ks[7], (DEPTH, 2, D_MODEL, D_FF), D_MODEL ** -0.5),
        'ffn_w_down': nrm(ks[8], (DEPTH, 2, D_FF, D_MODEL), D_FF ** -0.5),
        'w_in': nrm(ks[9], (DEPTH, D_MODEL, IN_WIDTH), D_MODEL ** -0.5),
        'w_out': nrm(ks[10], (DEPTH, MIX_WIDTH, D_MODEL), MIX_WIDTH ** -0.5),
        'mla_qa_g': gain(ks[11], (DEPTH, Q_LORA)),
        'mla_wqb': nrm(ks[12], (DEPTH, Q_LORA, MLA_HEADS * (QK_NOPE + QK_ROPE)), Q_LORA ** -0.5),
        'mla_kva_g': gain(ks[13], (DEPTH, KV_LORA)),
        'mla_w_uk': nrm(ks[14], (DEPTH, KV_LORA, MLA_HEADS, QK_NOPE), KV_LORA ** -0.5),
        'mla_w_uv': nrm(ks[15], (DEPTH, KV_LORA, MLA_HEADS, V_HEAD), KV_LORA ** -0.5),
        'hgrn_lb_logits': nrm(ks[16], (DEPTH, HGRN_KEY), 0.5),
        'hgrn_out_g': gain(ks[17], (DEPTH, HGRN_WIDTH)),
        'cmlp_v_g': gain(ks[18], (DEPTH, CMLP_WIDTH)),
        'cmlp_w_s': nrm(ks[19], (DEPTH, CMLP_GROUPS, CMLP_CHUNK, CMLP_CHUNK), CMLP_CHUNK ** -0.5),
        'cmlp_b_s': 1.0 + 0.1 * jax.random.normal(ks[20], (DEPTH, CMLP_GROUPS, CMLP_CHUNK), f32),
    }


def reference(x_prompt, x_sample, cache_mla_ckv, cache_mla_kpe, state_hgrn, norm_g, ffn_w_gate,
              ffn_w_up, ffn_w_down, w_in, w_out, mla_qa_g, mla_wqb, mla_kva_g, mla_w_uk, mla_w_uv,
              hgrn_lb_logits, hgrn_out_g, cmlp_v_g, cmlp_w_s, cmlp_b_s):
    p = dict(norm_g=norm_g, ffn_w_gate=ffn_w_gate, ffn_w_up=ffn_w_up, ffn_w_down=ffn_w_down,
             w_in=w_in, w_out=w_out, mla_qa_g=mla_qa_g, mla_wqb=mla_wqb, mla_kva_g=mla_kva_g,
             mla_w_uk=mla_w_uk, mla_w_uv=mla_w_uv, hgrn_lb_logits=hgrn_lb_logits,
             hgrn_out_g=hgrn_out_g, cmlp_v_g=cmlp_v_g, cmlp_w_s=cmlp_w_s, cmlp_b_s=cmlp_b_s)
    pos_p = jnp.arange(x_prompt.shape[1], dtype=jnp.int32)
    y_prompt, ckv_p, kpe_p, hgrn_p, _ = _trunk(x_prompt, pos_p, None, None, None, p)
    past = cache_mla_ckv.shape[2]
    pos_s = past + jnp.arange(x_sample.shape[1], dtype=jnp.int32)
    y_sample, ckv_s, kpe_s, hgrn_s, cmlp_v_s = _trunk(x_sample, pos_s, cache_mla_ckv, cache_mla_kpe, state_hgrn, p)
    return (y_prompt, y_sample, ckv_p, kpe_p, hgrn_p, ckv_s, kpe_s, hgrn_s, cmlp_v_s)
```

```python
import functools

import jax
import jax.numpy as jnp
from jax import lax
from jax.experimental import pallas as pl
from jax.experimental.pallas import tpu as pltpu

F32 = jnp.float32
BF16 = jnp.bfloat16

EPS = 1e-6
CHUNK = 64
MLA_HEADS = 4
Q_LORA = 384
KV_LORA = 256
QK_NOPE = 128
QK_ROPE = 64
V_HEAD = 128
MLA_SCALE = (QK_NOPE + QK_ROPE) ** -0.5
ROPE_BASE = 10000.0
HGRN_HEADS = 4
HGRN_HEAD_DIM = 64
GROUP = 64
WIDTH = 256
CMLP_GROUPS = 4
CMLP_CHUNK = 128
ROPE_PAD = 128
HGRN_BLOCK = 16
MASKED = -1e30

VMEM_LIMIT = 56 * 1024 * 1024


def _params(*sem):
    return pltpu.CompilerParams(dimension_semantics=sem, vmem_limit_bytes=VMEM_LIMIT)


def _dot(a, b):
    return jnp.dot(a, b, preferred_element_type=F32)


def _dot_nt(a, b):
    return lax.dot_general(a, b, (((1,), (1,)), ((), ())), preferred_element_type=F32)


def _dot_tn(a, b):
    return lax.dot_general(a, b, (((0,), (0,)), ((), ())), preferred_element_type=F32)


def _rms(x, g):
    return x * lax.rsqrt(jnp.mean(x * x, axis=-1, keepdims=True) + EPS) * g


def _silu(x):
    return x * jax.nn.sigmoid(x)


def _gelu_tanh(x):
    return 0.5 * x * (1.0 + jnp.tanh(0.7978845608028654 * (x + 0.044715 * (x * x * x))))


def _group_indicator():
    r = lax.broadcasted_iota(jnp.int32, (WIDTH, WIDTH), 0) // GROUP
    c = lax.broadcasted_iota(jnp.int32, (WIDTH, WIDTH), 1) // GROUP
    return r == c


def _group_rms(x, g, ind_b):
    ms = _dot((x * x).astype(BF16), ind_b) * (1.0 / GROUP)
    return x * lax.rsqrt(ms + EPS) * g


def _split3(x):
    hi = x.astype(BF16)
    r1 = x - hi.astype(F32)
    mid = r1.astype(BF16)
    lo = (r1 - mid.astype(F32)).astype(BF16)
    return hi, mid, lo


def _ffn_kernel(x_ref, gpre_ref, gpost_ref, wg_ref, wu_ref, wd_ref, o_ref, *, tf):
    x = x_ref[...]
    h = _rms(x, gpre_ref[...]).astype(BF16)
    d_ff = wg_ref.shape[1]
    acc = jnp.zeros(x.shape, F32)
    for c in range(d_ff // tf):
        sl = slice(c * tf, (c + 1) * tf)
        g = _dot(h, wg_ref[:, sl])
        u = _dot(h, wu_ref[:, sl])
        a = (_silu(g) * u).astype(BF16)
        acc = acc + _dot(a, wd_ref[sl, :])
    o_ref[...] = x + 0.5 * _rms(acc, gpost_ref[...])


def _ffn(x, norm_g, wg, wu, wd, l, j, tm):
    n, d = x.shape
    d_ff = wg.shape[-1]
    const = lambda r, c: pl.BlockSpec((None, None, r, c), lambda i: (l, j, 0, 0))
    gain = lambda k: pl.BlockSpec((None, None, 1, d), lambda i: (l, k, 0, 0))
    return pl.pallas_call(
        functools.partial(_ffn_kernel, tf=256),
        grid=(n // tm,),
        in_specs=[pl.BlockSpec((tm, d), lambda i: (i, 0)), gain(4 * j), gain(4 * j + 1),
                  const(d, d_ff), const(d, d_ff), const(d_ff, d)],
        out_specs=pl.BlockSpec((tm, d), lambda i: (i, 0)),
        out_shape=jax.ShapeDtypeStruct((n, d), F32),
        compiler_params=_params("parallel"),
        name="ffn",
    )(x, norm_g, norm_g, wg, wu, wd)


def _inproj_kernel(x_ref, g_ref, win_ref, qag_ref, kvag_ref, wqb_ref, wuk_ref, cos_ref, sin_ref,
                   qlat_ref, qpe_ref, c_ref, cb_ref, kpe_ref, kpb_ref,
                   hq_ref, hf_ref, hi_ref, hg_ref, cu_ref, cv_ref):
    h = _rms(x_ref[...], g_ref[...]).astype(BF16)
    z = _dot(h, win_ref[...])
    cos = cos_ref[...]
    sin = sin_ref[...]
    o = Q_LORA
    c_new = _rms(z[:, o:o + KV_LORA], kvag_ref[...])
    c_ref[...] = c_new
    cb_ref[...] = c_new.astype(BF16)
    o += KV_LORA
    for ref in (hq_ref, hf_ref, hi_ref, hg_ref, cu_ref, cv_ref):
        ref[...] = z[:, o:o + WIDTH]
        o += WIDTH
    kpe = z[:, o:o + ROPE_PAD] * cos + z[:, o + ROPE_PAD:o + 2 * ROPE_PAD] * sin
    kpe_ref[...] = kpe[:, :QK_ROPE]
    kpb_ref[...] = kpe.astype(BF16)
    qn = _rms(z[:, :Q_LORA], qag_ref[...]).astype(BF16)
    qq = _dot(qn, wqb_ref[...])
    pe0 = MLA_HEADS * QK_NOPE
    per0 = pe0 + MLA_HEADS * ROPE_PAD
    for hd in range(MLA_HEADS):
        pe = qq[:, pe0 + hd * ROPE_PAD:pe0 + (hd + 1) * ROPE_PAD]
        per = qq[:, per0 + hd * ROPE_PAD:per0 + (hd + 1) * ROPE_PAD]
        qpe_ref[hd] = ((pe * cos + per * sin) * MLA_SCALE).astype(BF16)
        nope = qq[:, hd * QK_NOPE:(hd + 1) * QK_NOPE].astype(BF16)
        qlat_ref[hd] = (_dot(nope, wuk_ref[hd]) * MLA_SCALE).astype(BF16)


def _inproj(x, norm_g, l, w, cos, sin, tm):
    n, d = x.shape
    nt = cos.shape[0] // tm
    row = lambda wd, dt: jax.ShapeDtypeStruct((n, wd), dt)
    rspec = lambda wd: pl.BlockSpec((tm, wd), lambda i: (i, 0))
    hspec = lambda wd: pl.BlockSpec((MLA_HEADS, tm, wd), lambda i: (0, i, 0))
    full = lambda a: pl.BlockSpec((None,) + a.shape[1:], lambda i: (l,) + (0,) * (a.ndim - 1))
    return pl.pallas_call(
        _inproj_kernel,
        grid=(n // tm,),
        in_specs=[
            rspec(d),
            pl.BlockSpec((None, None, 1, d), lambda i: (l, 2, 0, 0)),
            full(w["win"]), full(w["qa_g"]), full(w["kva_g"]), full(w["wqb"]), full(w["wuk"]),
            pl.BlockSpec((tm, ROPE_PAD), lambda i: (i % nt, 0)),
            pl.BlockSpec((tm, ROPE_PAD), lambda i: (i % nt, 0)),
        ],
        out_specs=[hspec(KV_LORA), hspec(ROPE_PAD), rspec(KV_LORA), rspec(KV_LORA),
                   rspec(QK_ROPE), rspec(ROPE_PAD)] + [rspec(WIDTH)] * 6,
        out_shape=[jax.ShapeDtypeStruct((MLA_HEADS, n, KV_LORA), BF16),
                   jax.ShapeDtypeStruct((MLA_HEADS, n, ROPE_PAD), BF16),
                   row(KV_LORA, F32), row(KV_LORA, BF16), row(QK_ROPE, F32), row(ROPE_PAD, BF16)]
        + [row(WIDTH, F32)] * 6,
        compiler_params=_params("parallel"),
        name="in_proj",
    )(x, norm_g, w["win"], w["qa_g"], w["kva_g"], w["wqb"], w["wuk"], cos, sin)


def _attn_finish(acc, l, wuv_ref, o_ref, tq):
    o_lat = (acc / l).astype(BF16)
    for hd in range(MLA_HEADS):
        o_ref[:, hd * V_HEAD:(hd + 1) * V_HEAD] = _dot(
            o_lat[hd * tq:(hd + 1) * tq], wuv_ref[hd]).astype(o_ref.dtype)


def _attn_prompt_kernel(qlat_ref, qpe_ref, kc_ref, kp_ref, wuv_ref, o_ref, *, tq):
    i = pl.program_id(1)
    rows = MLA_HEADS * tq
    q = qlat_ref[...].reshape(rows, KV_LORA)
    qp = qpe_ref[...].reshape(rows, ROPE_PAD)

    def scores(j):
        ks = pl.ds(pl.multiple_of(j * tq, tq), tq)
        kc = kc_ref[ks, :]
        return _dot_nt(q, kc) + _dot_nt(qp, kp_ref[ks, :]), kc

    def update(s, kc, carry):
        m, l, acc = carry
        m_new = jnp.maximum(m, jnp.max(s, axis=-1, keepdims=True))
        alpha = jnp.exp(m - m_new)
        p = jnp.exp(s - m_new)
        l = alpha * l + jnp.sum(p, axis=-1, keepdims=True)
        acc = alpha * acc + _dot(p.astype(BF16), kc)
        return m_new, l, acc

    def body(j, carry):
        s, kc = scores(j)
        return update(s, kc, carry)

    init = (jnp.full((rows, 1), -jnp.inf, F32), jnp.zeros((rows, 1), F32),
            jnp.zeros((rows, KV_LORA), F32))
    carry = lax.fori_loop(0, i, body, init)
    s, kc = scores(i)
    qc = (lax.broadcasted_iota(jnp.int32, (rows, tq), 0) % tq) // CHUNK
    kch = lax.broadcasted_iota(jnp.int32, (rows, tq), 1) // CHUNK
    s = jnp.where(kch <= qc, s, -jnp.inf)
    _, l, acc = update(s, kc, carry)
    _attn_finish(acc, l, wuv_ref, o_ref, tq)


def _attn_prompt(qlat, qpe, kc, kp, wuv, l, batch, seq, tq):
    n = batch * seq
    nq = seq // tq
    return pl.pallas_call(
        functools.partial(_attn_prompt_kernel, tq=tq),
        grid=(batch, nq),
        in_specs=[
            pl.BlockSpec((MLA_HEADS, tq, KV_LORA), lambda b, i: (0, b * nq + i, 0)),
            pl.BlockSpec((MLA_HEADS, tq, ROPE_PAD), lambda b, i: (0, b * nq + i, 0)),
            pl.BlockSpec((seq, KV_LORA), lambda b, i: (b, 0)),
            pl.BlockSpec((seq, ROPE_PAD), lambda b, i: (b, 0)),
            pl.BlockSpec((None,) + wuv.shape[1:], lambda b, i: (l, 0, 0, 0)),
        ],
        out_specs=pl.BlockSpec((tq, MLA_HEADS * V_HEAD), lambda b, i: (b * nq + i, 0)),
        out_shape=jax.ShapeDtypeStruct((n, MLA_HEADS * V_HEAD), BF16),
        compiler_params=_params("parallel", "arbitrary"),
        name="attn_prompt",
    )(qlat, qpe, kc, kp, wuv)


def _attn_sample_kernel(qlat_ref, qpe_ref, ckv_ref, ckpe_ref, kc_ref, kp_ref, wuv_ref, o_ref, *, tq):
    rows = MLA_HEADS * tq
    q = qlat_ref[...].reshape(rows, KV_LORA)
    qp = qpe_ref[...].reshape(rows, ROPE_PAD)
    c_past = ckv_ref[...].astype(BF16)
    p_past = ckpe_ref[...].astype(BF16)
    kc = kc_ref[...]
    s_past = _dot_nt(q, c_past) + _dot_nt(qp[:, :QK_ROPE], p_past)
    s_new = _dot_nt(q, kc) + _dot_nt(qp, kp_ref[...])
    m = jnp.maximum(jnp.max(s_past, axis=-1, keepdims=True), jnp.max(s_new, axis=-1, keepdims=True))
    e_past = jnp.exp(s_past - m)
    e_new = jnp.exp(s_new - m)
    l = jnp.sum(e_past, axis=-1, keepdims=True) + jnp.sum(e_new, axis=-1, keepdims=True)
    acc = _dot(e_past.astype(BF16), c_past) + _dot(e_new.astype(BF16), kc)
    _attn_finish(acc, l, wuv_ref, o_ref, tq)


def _attn_sample(qlat, qpe, cache_ckv, cache_kpe, kc, kp, wuv, l, batch, seq):
    n = batch * seq
    past = cache_ckv.shape[2]
    return pl.pallas_call(
        functools.partial(_attn_sample_kernel, tq=seq),
        grid=(batch,),
        in_specs=[
            pl.BlockSpec((MLA_HEADS, seq, KV_LORA), lambda b: (0, b, 0)),
            pl.BlockSpec((MLA_HEADS, seq, ROPE_PAD), lambda b: (0, b, 0)),
            pl.BlockSpec((None, None, past, KV_LORA), lambda b: (l, b, 0, 0)),
            pl.BlockSpec((None, None, past, QK_ROPE), lambda b: (l, b, 0, 0)),
            pl.BlockSpec((seq, KV_LORA), lambda b: (b, 0)),
            pl.BlockSpec((seq, ROPE_PAD), lambda b: (b, 0)),
            pl.BlockSpec((None,) + wuv.shape[1:], lambda b: (l, 0, 0, 0)),
        ],
        out_specs=pl.BlockSpec((seq, MLA_HEADS * V_HEAD), lambda b: (b, 0)),
        out_shape=jax.ShapeDtypeStruct((n, MLA_HEADS * V_HEAD), BF16),
        compiler_params=_params("parallel"),
        name="attn_sample",
    )(qlat, qpe, cache_ckv, cache_kpe, kc, kp, wuv)


def _hgrn_lower_bound(lg, layer, depth):
    rows = [lg[r:r + 1] for r in range(depth)]
    mx = functools.reduce(jnp.maximum, rows)
    ex = [jnp.exp(r - mx) for r in rows]
    tot = functools.reduce(lambda a, b: a + b, ex)
    sm = [e / tot for e in ex]
    cum0 = sm[0]
    cum = functools.reduce(lambda a, b: a + b, sm[:layer + 1])
    return cum - cum0


def _hgrn_kernel(*refs, layer, depth, has_state, tt):
    hq_ref, hf_ref, hi_ref, hg_ref, lbl_ref, og_ref = refs[:6]
    refs = refs[6:]
    if has_state:
        s0_ref, refs = refs[0], refs[1:]
    o_ref, sfin_ref, st_ref, qd_ref, kd_ref, dec_ref, k_ref, b_ref, oacc_ref = refs
    t = pl.program_id(1)
    blk = HGRN_BLOCK
    ind = _group_indicator()
    ind_b = ind.astype(BF16)

    @pl.when(t == 0)
    def _():
        if has_state:
            s0 = s0_ref[...].reshape(WIDTH, HGRN_HEAD_DIM)
            tiled = jnp.concatenate([s0] * HGRN_HEADS, axis=1)
            st_ref[...] = jnp.where(ind, tiled, 0.0).T
        else:
            st_ref[...] = jnp.zeros((WIDTH, WIDTH), F32)

    lb = _hgrn_lower_bound(lbl_ref[...], layer, depth)
    zf = hf_ref[...]
    log_sig = jnp.minimum(zf, 0.0) - jnp.log1p(jnp.exp(-jnp.abs(zf)))
    a = jnp.log(lb)
    b = jnp.log1p(-lb) + log_sig
    logf = jnp.maximum(a, b) + jnp.log1p(jnp.exp(-jnp.abs(a - b)))
    k = (1.0 - lb) * jax.nn.sigmoid(-zf)

    r_i = lax.broadcasted_iota(jnp.int32, (tt, tt), 0)
    c_i = lax.broadcasted_iota(jnp.int32, (tt, tt), 1)
    same = (r_i // blk) == (c_i // blk)
    tri = (same & (c_i <= r_i)).astype(BF16)
    ones = same.astype(BF16)
    hi, mid, lo = _split3(logf)
    bloc = _dot(tri, hi) + _dot(tri, mid) + _dot(tri, lo)
    btot = _dot(ones, hi) + _dot(ones, mid) + _dot(ones, lo)
    qd_ref[...] = hq_ref[...] * jnp.exp(bloc)
    kd_ref[...] = k * jnp.exp(btot - bloc)
    dec_ref[...] = jnp.exp(btot)
    k_ref[...] = k
    b_ref[...] = bloc

    row_i = lax.broadcasted_iota(jnp.int32, (blk, WIDTH), 0)

    def body(i, carry):
        rs = pl.ds(pl.multiple_of(i * blk, blk), blk)
        st = st_ref[...]
        v = hi_ref[rs, :]
        o_inter = _dot_nt(qd_ref[rs, :].astype(BF16), st.astype(BF16))
        kv_t = _dot_tn(v.astype(BF16), kd_ref[rs, :].astype(BF16))
        st_ref[...] = st * dec_ref[rs, :][0:1] + jnp.where(ind, kv_t, 0.0)
        q = hq_ref[rs, :]
        kk = k_ref[rs, :]
        bl = b_ref[rs, :]
        parts = []
        for s in range(blk):
            bs = jnp.broadcast_to(bl[s:s + 1], (blk, WIDTH))
            ks = jnp.broadcast_to(kk[s:s + 1], (blk, WIDTH))
            parts.append(q * ks * jnp.exp(jnp.where(row_i >= s, bl - bs, MASKED)))
        terms = jnp.concatenate(parts, axis=0).astype(BF16)
        a_bc = _dot(terms, ind_b)
        o_intra = jnp.zeros((blk, WIDTH), F32)
        for s in range(blk):
            o_intra = o_intra + a_bc[s * blk:(s + 1) * blk] * jnp.broadcast_to(v[s:s + 1], (blk, WIDTH))
        oacc_ref[rs, :] = o_inter + o_intra
        return carry

    lax.fori_loop(0, tt // blk, body, 0)
    o = _group_rms(oacc_ref[...], og_ref[...], ind_b) * _silu(hg_ref[...])
    o_ref[...] = o.astype(o_ref.dtype)

    @pl.when(t == pl.num_programs(1) - 1)
    def _():
        s_full = st_ref[...].T
        for hd in range(HGRN_HEADS):
            sl = slice(hd * GROUP, (hd + 1) * GROUP)
            sfin_ref[hd] = s_full[sl, sl]


def _hgrn(hq, hf, hi, hg, lb_logits, out_g, state, l, batch, seq, tt):
    n = batch * seq
    nt = seq // tt
    depth = lb_logits.shape[0]
    rspec = pl.BlockSpec((tt, WIDTH), lambda b, t: (b * nt + t, 0))
    in_specs = [rspec] * 4 + [
        pl.BlockSpec((depth, WIDTH), lambda b, t: (0, 0)),
        pl.BlockSpec((None, 1, WIDTH), lambda b, t: (l, 0, 0)),
    ]
    args = [hq, hf, hi, hg, lb_logits, out_g]
    sshape = (HGRN_HEADS, GROUP, HGRN_HEAD_DIM)
    if state is not None:
        in_specs.append(pl.BlockSpec((None, None) + sshape, lambda b, t: (l, b, 0, 0, 0)))
        args.append(state)
    return pl.pallas_call(
        functools.partial(_hgrn_kernel, layer=l, depth=depth, has_state=state is not None, tt=tt),
        grid=(batch, nt),
        in_specs=in_specs,
        out_specs=[rspec, pl.BlockSpec((None,) + sshape, lambda b, t: (b, 0, 0, 0))],
        out_shape=[jax.ShapeDtypeStruct((n, WIDTH), BF16),
                   jax.ShapeDtypeStruct((batch,) + sshape, F32)],
        scratch_shapes=[pltpu.VMEM((WIDTH, WIDTH), F32)] + [pltpu.VMEM((tt, WIDTH), F32)] * 6,
        compiler_params=_params("parallel", "arbitrary"),
        name="hgrn",
    )(*args)


def _out_kernel(oa_ref, ob_ref, cu_ref, cv_ref, x_ref, vg_ref, ws_ref, bias_ref, wout_ref, g_ref,
                *out_refs, lc, emit_v):
    y_ref = out_refs[0]
    tm = x_ref.shape[0]
    ind_b = _group_indicator().astype(BF16)
    u = _gelu_tanh(cu_ref[...])
    v = _group_rms(_gelu_tanh(cv_ref[...]), vg_ref[...], ind_b)
    if emit_v:
        out_refs[1][...] = v
    vb = v.astype(BF16)
    r_i = lax.broadcasted_iota(jnp.int32, (lc, CMLP_GROUPS * lc), 0)
    c_i = lax.broadcasted_iota(jnp.int32, (lc, CMLP_GROUPS * lc), 1)
    w = jnp.where((c_i % lc) <= r_i, ws_ref[...], 0.0).astype(BF16)
    lane_g = lax.broadcasted_iota(jnp.int32, (lc, WIDTH), 1) // GROUP
    bias = bias_ref[...]
    mixed = []
    for c in range(tm // lc):
        vc = vb[c * lc:(c + 1) * lc]
        v_exp = jnp.concatenate([jnp.where(lane_g == g, vc, 0) for g in range(CMLP_GROUPS)], axis=0)
        mixed.append(_dot(w, v_exp) + bias)
    o_c = (u * jnp.concatenate(mixed, axis=0)).astype(BF16)
    na = oa_ref.shape[1]
    nb = na + WIDTH
    mix = (_dot(oa_ref[...], wout_ref[:na, :]) + _dot(ob_ref[...], wout_ref[na:nb, :])
           + _dot(o_c, wout_ref[nb:, :]))
    y_ref[...] = x_ref[...] + _rms(mix, g_ref[...])


def _out(oa, ob, cu, cv, x, norm_g, w, l, lc, emit_v, tm):
    n, d = x.shape
    rspec = lambda wd: pl.BlockSpec((tm, wd), lambda i: (i, 0))
    full = lambda a: pl.BlockSpec((None,) + a.shape[1:], lambda i: (l,) + (0,) * (a.ndim - 1))
    out_specs = [rspec(d)]
    out_shape = [jax.ShapeDtypeStruct((n, d), F32)]
    if emit_v:
        out_specs.append(rspec(WIDTH))
        out_shape.append(jax.ShapeDtypeStruct((n, WIDTH), F32))
    return pl.pallas_call(
        functools.partial(_out_kernel, lc=lc, emit_v=emit_v),
        grid=(n // tm,),
        in_specs=[rspec(oa.shape[1]), rspec(WIDTH), rspec(WIDTH), rspec(WIDTH), rspec(d),
                  full(w["v_g"]), full(w["ws"]), full(w["bias"]), full(w["wout"]),
                  pl.BlockSpec((None, None, 1, d), lambda i: (l, 3, 0, 0))],
        out_specs=out_specs,
        out_shape=out_shape,
        compiler_params=_params("parallel"),
        name="out_proj",
    )(oa, ob, cu, cv, x, w["v_g"], w["ws"], w["bias"], w["wout"], norm_g)


def _rot_cols(w):
    half = QK_ROPE // 2
    return jnp.concatenate([-w[..., half:], w[..., :half]], axis=-1)


def _pad_rope(w):
    return jnp.pad(w, [(0, 0)] * (w.ndim - 1) + [(0, ROPE_PAD - QK_ROPE)])


def _prep_weights(w_in, w_out, mla_qa_g, mla_wqb, mla_kva_g, mla_w_uk, mla_w_uv, hgrn_out_g, cmlp_v_g):
    depth = w_in.shape[0]
    o_kpe = Q_LORA + KV_LORA
    w_kpe = w_in[:, :, o_kpe:o_kpe + QK_ROPE]
    win = jnp.concatenate([w_in[:, :, :o_kpe], w_in[:, :, o_kpe + QK_ROPE:],
                           _pad_rope(w_kpe), _pad_rope(_rot_cols(w_kpe))], axis=-1).astype(BF16)
    wqb = mla_wqb.reshape(depth, Q_LORA, MLA_HEADS, QK_NOPE + QK_ROPE)
    nope = wqb[..., :QK_NOPE].reshape(depth, Q_LORA, MLA_HEADS * QK_NOPE)
    pe = wqb[..., QK_NOPE:]
    flat = lambda a: _pad_rope(a).reshape(depth, Q_LORA, MLA_HEADS * ROPE_PAD)
    wqb = jnp.concatenate([nope, flat(pe), flat(_rot_cols(pe))], axis=-1).astype(BF16)
    return dict(
        win=win, wqb=wqb,
        wuk=jnp.transpose(mla_w_uk, (0, 2, 3, 1)).astype(BF16),
        wuv=jnp.transpose(mla_w_uv, (0, 2, 1, 3)).astype(BF16),
        wout=w_out.astype(BF16),
        qa_g=mla_qa_g[:, None, :], kva_g=mla_kva_g[:, None, :],
        out_g=hgrn_out_g[:, None, :], v_g=cmlp_v_g[:, None, :],
    )


def _rope_tables(pos, rows):
    inv_freq = ROPE_BASE ** (-jnp.arange(0, QK_ROPE, 2, dtype=F32) / QK_ROPE)
    ang = pos.astype(F32)[:, None] * inv_freq[None, :]
    pad = jnp.zeros((pos.shape[0], ROPE_PAD - QK_ROPE), F32)
    cos = jnp.concatenate([jnp.cos(ang), jnp.cos(ang), pad], axis=-1)
    sin = jnp.concatenate([jnp.sin(ang), jnp.sin(ang), pad], axis=-1)
    reps = max(1, rows // pos.shape[0])
    return jnp.tile(cos, (reps, 1)), jnp.tile(sin, (reps, 1))


def _tile(n, want):
    return want if n % want == 0 else n


def _trunk(x3, pos, cache_ckv, cache_kpe, state, norm_g, ffn_w, w, lb_logits, cmlp_w_s, cmlp_b_s):
    batch, seq, d = x3.shape
    n = batch * seq
    depth = norm_g.shape[0]
    x = x3.reshape(n, d)
    tm = _tile(n, 512)
    cos, sin = _rope_tables(pos, tm)
    lc = min(CMLP_CHUNK, seq)
    ws = jnp.transpose(cmlp_w_s[:, :, :lc, :lc], (0, 2, 1, 3)).reshape(depth, lc, CMLP_GROUPS * lc)
    bias = jnp.repeat(jnp.transpose(cmlp_b_s[:, :, :lc], (0, 2, 1)), GROUP, axis=-1)
    w = dict(w, ws=ws, bias=bias)
    is_sample = cache_ckv is not None
    ckv_rows, kpe_rows, states, v_rows = [], [], [], []
    for l in range(depth):
        x = _ffn(x, norm_g, *ffn_w, l, 0, tm)
        (qlat, qpe, c_new, c_b, kpe_new, kpe_b, hq, hf, hi, hg, cu, cv) = _inproj(
            x, norm_g, l, w, cos, sin, tm)
        if is_sample:
            o_a = _attn_sample(qlat, qpe, cache_ckv, cache_kpe, c_b, kpe_b, w["wuv"], l, batch, seq)
        else:
            o_a = _attn_prompt(qlat, qpe, c_b, kpe_b, w["wuv"], l, batch, seq, _tile(seq, 256))
        o_b, s_new = _hgrn(hq, hf, hi, hg, lb_logits, w["out_g"], state, l, batch, seq,
                           _tile(seq, 256))
        outs = _out(o_a, o_b, cu, cv, x, norm_g, w, l, lc, is_sample, tm)
        x = outs[0]
        x = _ffn(x, norm_g, *ffn_w, l, 1, tm)
        ckv_rows.append(c_new.reshape(batch, seq, KV_LORA))
        kpe_rows.append(kpe_new.reshape(batch, seq, QK_ROPE))
        states.append(s_new)
        if is_sample:
            v_rows.append(outs[1].reshape(batch, seq, WIDTH))
    res = [x.reshape(batch, seq, d), jnp.stack(ckv_rows), jnp.stack(kpe_rows), jnp.stack(states)]
    if is_sample:
        res.append(jnp.stack(v_rows))
    return res


def kernel(x_prompt, x_sample, cache_mla_ckv, cache_mla_kpe, state_hgrn, norm_g, ffn_w_gate, ffn_w_up,
           ffn_w_down, w_in, w_out, mla_qa_g, mla_wqb, mla_kva_g, mla_w_uk, mla_w_uv, hgrn_lb_logits,
           hgrn_out_g, cmlp_v_g, cmlp_w_s, cmlp_b_s):
    w = _prep_weights(w_in, w_out, mla_qa_g, mla_wqb, mla_kva_g, mla_w_uk, mla_w_uv, hgrn_out_g,
                      cmlp_v_g)
    ffn_w = (ffn_w_gate.astype(BF16), ffn_w_up.astype(BF16), ffn_w_down.astype(BF16))
    depth = norm_g.shape[0]
    ng = norm_g.reshape(depth, norm_g.shape[1], 1, norm_g.shape[2])
    common = (ng, ffn_w, w, hgrn_lb_logits, cmlp_w_s, cmlp_b_s)
    pos_p = jnp.arange(x_prompt.shape[1], dtype=jnp.int32)
    y_p, ckv_p, kpe_p, hgrn_p = _trunk(x_prompt, pos_p, None, None, None, *common)
    pos_s = cache_mla_ckv.shape[2] + jnp.arange(x_sample.shape[1], dtype=jnp.int32)
    y_s, ckv_s, kpe_s, hgrn_s, v_s = _trunk(x_sample, pos_s, cache_mla_ckv, cache_mla_kpe, state_hgrn,
                                            *common)
    return (y_p, y_s, ckv_p, kpe_p, hgrn_p, ckv_s, kpe_s, hgrn_s, v_s)
```

```python
import functools

import jax
import jax.numpy as jnp
from jax import lax
from jax.experimental import pallas as pl
from jax.experimental.pallas import tpu as pltpu

F32 = jnp.float32
BF16 = jnp.bfloat16

EPS = 1e-6
CHUNK = 64
MLA_HEADS = 4
Q_LORA = 384
KV_LORA = 256
QK_NOPE = 128
QK_ROPE = 64
V_HEAD = 128
MLA_SCALE = (QK_NOPE + QK_ROPE) ** -0.5
ROPE_BASE = 10000.0
HGRN_HEADS = 4
HGRN_HEAD_DIM = 64
GROUP = 64
WIDTH = 256
CMLP_GROUPS = 4
CMLP_CHUNK = 128
LANES = 128
ROPE_PAD = LANES
QK_CAT = KV_LORA + ROPE_PAD
Q_SCALE = MLA_SCALE * 1.4426950408889634
HGRN_BLOCK = 16
MASKED = -1e30

VMEM_LIMIT = 56 * 1024 * 1024


def _params(*sem):
    return pltpu.CompilerParams(dimension_semantics=sem, vmem_limit_bytes=VMEM_LIMIT)


def _dot(a, b):
    return jnp.dot(a, b, preferred_element_type=F32)


def _dot_nt(a, b):
    return lax.dot_general(a, b, (((1,), (1,)), ((), ())), preferred_element_type=F32)


def _dot_tn(a, b):
    return lax.dot_general(a, b, (((0,), (0,)), ((), ())), preferred_element_type=F32)


def _rms(x, g):
    return x * lax.rsqrt(jnp.mean(x * x, axis=-1, keepdims=True) + EPS) * g


def _silu(x):
    return x * jax.nn.sigmoid(x)


def _gelu_tanh(x):
    return 0.5 * x * (1.0 + jnp.tanh(0.7978845608028654 * (x + 0.044715 * (x * x * x))))


def _group_indicator():
    r = lax.broadcasted_iota(jnp.int32, (WIDTH, WIDTH), 0) // GROUP
    c = lax.broadcasted_iota(jnp.int32, (WIDTH, WIDTH), 1) // GROUP
    return r == c


def _group_rms(x, g, ind_b):
    ms = _dot((x * x).astype(BF16), ind_b) * (1.0 / GROUP)
    return x * lax.rsqrt(ms + EPS) * g


def _split3(x):
    hi = x.astype(BF16)
    r1 = x - hi.astype(F32)
    mid = r1.astype(BF16)
    lo = (r1 - mid.astype(F32)).astype(BF16)
    return hi, mid, lo


def _ffn_kernel(x_ref, gpre_ref, gpost_ref, wg_ref, wu_ref, wd_ref, o_ref, *, tf):
    x = x_ref[...]
    h = _rms(x, gpre_ref[...]).astype(BF16)
    d_ff = wg_ref.shape[1]
    acc = jnp.zeros(x.shape, F32)
    for c in range(d_ff // tf):
        sl = slice(c * tf, (c + 1) * tf)
        g = _dot(h, wg_ref[:, sl])
        u = _dot(h, wu_ref[:, sl])
        a = (_silu(g) * u).astype(BF16)
        acc = acc + _dot(a, wd_ref[sl, :])
    o_ref[...] = x + 0.5 * _rms(acc, gpost_ref[...])


def _ffn(x, norm_g, wg, wu, wd, l, j, tm):
    n, d = x.shape
    d_ff = wg.shape[-1]
    const = lambda r, c: pl.BlockSpec((None, None, r, c), lambda i: (l, j, 0, 0))
    gain = lambda k: pl.BlockSpec((None, None, 1, d), lambda i: (l, k, 0, 0))
    return pl.pallas_call(
        functools.partial(_ffn_kernel, tf=256),
        grid=(n // tm,),
        in_specs=[pl.BlockSpec((tm, d), lambda i: (i, 0)), gain(4 * j), gain(4 * j + 1),
                  const(d, d_ff), const(d, d_ff), const(d_ff, d)],
        out_specs=pl.BlockSpec((tm, d), lambda i: (i, 0)),
        out_shape=jax.ShapeDtypeStruct((n, d), F32),
        compiler_params=_params("parallel"),
        name="ffn",
    )(x, norm_g, norm_g, wg, wu, wd)


def _inproj_kernel(x_ref, g_ref, win_ref, qag_ref, kvag_ref, wqb_ref, wuk_ref, cos_ref, sin_ref,
                   q_ref, c_ref, kpe_ref, kcat_ref, hq_ref, hf_ref, hi_ref, hg_ref, cu_ref, cv_ref,
                   ct_ref=None):
    h = _rms(x_ref[...], g_ref[...]).astype(BF16)
    z = _dot(h, win_ref[...])
    cos = cos_ref[...]
    sin = sin_ref[...]
    o = Q_LORA
    c_new = _rms(z[:, o:o + KV_LORA], kvag_ref[...])
    c_ref[...] = c_new
    kcat_ref[:, :KV_LORA] = c_new.astype(BF16)
    if ct_ref is not None:
        tk = ct_ref.shape[2]
        for u in range(ct_ref.shape[0]):
            ct_ref[u] = c_new[u * tk:(u + 1) * tk].T.astype(BF16)
    o += KV_LORA
    for ref in (hq_ref, hf_ref, hi_ref, hg_ref, cu_ref, cv_ref):
        ref[...] = z[:, o:o + WIDTH]
        o += WIDTH
    kpe = z[:, o:o + ROPE_PAD] * cos + z[:, o + ROPE_PAD:o + 2 * ROPE_PAD] * sin
    kpe_ref[...] = kpe[:, :QK_ROPE]
    kcat_ref[:, KV_LORA:] = kpe.astype(BF16)
    qn = _rms(z[:, :Q_LORA], qag_ref[...]).astype(BF16)
    qq = _dot(qn, wqb_ref[...])
    pe0 = MLA_HEADS * QK_NOPE
    per0 = pe0 + MLA_HEADS * ROPE_PAD
    for hd in range(MLA_HEADS):
        pe = qq[:, pe0 + hd * ROPE_PAD:pe0 + (hd + 1) * ROPE_PAD]
        per = qq[:, per0 + hd * ROPE_PAD:per0 + (hd + 1) * ROPE_PAD]
        q_ref[hd, :, KV_LORA:] = ((pe * cos + per * sin) * Q_SCALE).astype(BF16)
        nope = qq[:, hd * QK_NOPE:(hd + 1) * QK_NOPE].astype(BF16)
        q_ref[hd, :, :KV_LORA] = (_dot(nope, wuk_ref[hd]) * Q_SCALE).astype(BF16)


def _inproj(x, norm_g, l, w, cos, sin, tm, tk):
    n, d = x.shape
    nt = cos.shape[0] // tm
    row = lambda wd, dt: jax.ShapeDtypeStruct((n, wd), dt)
    rspec = lambda wd: pl.BlockSpec((tm, wd), lambda i: (i, 0))
    full = lambda a: pl.BlockSpec((None,) + a.shape[1:], lambda i: (l,) + (0,) * (a.ndim - 1))
    out_specs = [pl.BlockSpec((MLA_HEADS, tm, QK_CAT), lambda i: (0, i, 0)),
                 rspec(KV_LORA), rspec(QK_ROPE), rspec(QK_CAT)] + [rspec(WIDTH)] * 6
    out_shape = [jax.ShapeDtypeStruct((MLA_HEADS, n, QK_CAT), BF16),
                 row(KV_LORA, F32), row(QK_ROPE, F32), row(QK_CAT, BF16)] + [row(WIDTH, F32)] * 6
    if tk is not None:
        out_specs.append(pl.BlockSpec((tm // tk, KV_LORA, tk), lambda i: (i, 0, 0)))
        out_shape.append(jax.ShapeDtypeStruct((n // tk, KV_LORA, tk), BF16))
    return pl.pallas_call(
        _inproj_kernel,
        grid=(n // tm,),
        in_specs=[
            rspec(d),
            pl.BlockSpec((None, None, 1, d), lambda i: (l, 2, 0, 0)),
            full(w["win"]), full(w["qa_g"]), full(w["kva_g"]), full(w["wqb"]), full(w["wuk"]),
            pl.BlockSpec((tm, ROPE_PAD), lambda i: (i % nt, 0)),
            pl.BlockSpec((tm, ROPE_PAD), lambda i: (i % nt, 0)),
        ],
        out_specs=out_specs,
        out_shape=out_shape,
        compiler_params=_params("parallel"),
        name="in_proj",
    )(x, norm_g, w["win"], w["qa_g"], w["kva_g"], w["wqb"], w["wuk"], cos, sin)


def _attn_prompt_kernel(q_ref, kcat_ref, ct_ref, wuv_ref, o_ref, m_ref, l_ref, acc_ref, *, tq):
    i = pl.program_id(1)
    heads = range(MLA_HEADS)
    m_ref[...] = jnp.full(m_ref.shape, -jnp.inf, F32)
    l_ref[...] = jnp.zeros(l_ref.shape, F32)
    acc_ref[...] = jnp.zeros(acc_ref.shape, F32)

    def scores(j, hd):
        kcat = kcat_ref[pl.ds(pl.multiple_of(j * tq, tq), tq), :]
        return _dot_nt(kcat, q_ref[hd])

    def softmax_pv(s, j, hd, diagonal):
        if diagonal:
            kch = lax.broadcasted_iota(jnp.int32, (tq, tq), 0) // CHUNK
            qc = lax.broadcasted_iota(jnp.int32, (tq, tq), 1) // CHUNK
            s = jnp.where(kch <= qc, s, -jnp.inf)
        m_prev = m_ref[hd]
        m_new = jnp.maximum(m_prev, jnp.max(s, axis=0, keepdims=True))
        alpha = jnp.exp2(m_prev - m_new)
        p = jnp.exp2(s - m_new)
        l_ref[hd] = alpha * l_ref[hd] + jnp.sum(p, axis=0, keepdims=True)
        acc_ref[hd] = alpha * acc_ref[hd] + _dot(ct_ref[j], p.astype(BF16))
        m_ref[hd] = m_new

    s0 = tuple(scores(0, hd) for hd in heads)

    def body(j, s_cur):
        s_next = []
        for hd in heads:
            s_next.append(scores(j + 1, hd))
            softmax_pv(s_cur[hd], j, hd, False)
        return tuple(s_next)

    s_last = lax.fori_loop(0, i, body, s0)
    for hd in heads:
        softmax_pv(s_last[hd], i, hd, True)
    for hd in heads:
        o_lat_t = (acc_ref[hd] / l_ref[hd]).astype(BF16)
        o_ref[:, hd * V_HEAD:(hd + 1) * V_HEAD] = _dot_tn(o_lat_t, wuv_ref[hd]).astype(o_ref.dtype)


def _attn_prompt(q, kcat, ct, wuv, l, batch, seq, tq):
    n = batch * seq
    nq = seq // tq
    return pl.pallas_call(
        functools.partial(_attn_prompt_kernel, tq=tq),
        grid=(batch, nq),
        in_specs=[
            pl.BlockSpec((MLA_HEADS, tq, QK_CAT), lambda b, i: (0, b * nq + i, 0)),
            pl.BlockSpec((seq, QK_CAT), lambda b, i: (b, 0)),
            pl.BlockSpec((nq, KV_LORA, tq), lambda b, i: (b, 0, 0)),
            pl.BlockSpec((None,) + wuv.shape[1:], lambda b, i: (l, 0, 0, 0)),
        ],
        out_specs=pl.BlockSpec((tq, MLA_HEADS * V_HEAD), lambda b, i: (b * nq + i, 0)),
        out_shape=jax.ShapeDtypeStruct((n, MLA_HEADS * V_HEAD), BF16),
        scratch_shapes=[pltpu.VMEM((MLA_HEADS, 1, tq), F32), pltpu.VMEM((MLA_HEADS, 1, tq), F32),
                        pltpu.VMEM((MLA_HEADS, KV_LORA, tq), F32)],
        compiler_params=_params("parallel", "arbitrary"),
        name="attn_prompt",
    )(q, kcat, ct, wuv)


def _attn_sample_kernel(q_ref, ckv_ref, ckpe_ref, kcat_ref, wuv_ref, o_ref, *, tq):
    rows = MLA_HEADS * tq
    q = q_ref[...].reshape(rows, QK_CAT)
    c_past = ckv_ref[...].astype(BF16)
    p_past = ckpe_ref[...].astype(BF16)
    kcat = kcat_ref[...]
    kc = kcat[:, :KV_LORA]
    s_past = _dot_nt(q[:, :KV_LORA], c_past) + _dot_nt(q[:, KV_LORA:KV_LORA + QK_ROPE], p_past)
    s_new = _dot_nt(q, kcat)
    m = jnp.maximum(jnp.max(s_past, axis=-1, keepdims=True), jnp.max(s_new, axis=-1, keepdims=True))
    e_past = jnp.exp2(s_past - m)
    e_new = jnp.exp2(s_new - m)
    l = jnp.sum(e_past, axis=-1, keepdims=True) + jnp.sum(e_new, axis=-1, keepdims=True)
    acc = _dot(e_past.astype(BF16), c_past) + _dot(e_new.astype(BF16), kc)
    o_lat = (acc / l).astype(BF16)
    for hd in range(MLA_HEADS):
        o_ref[:, hd * V_HEAD:(hd + 1) * V_HEAD] = _dot(
            o_lat[hd * tq:(hd + 1) * tq], wuv_ref[hd]).astype(o_ref.dtype)


def _attn_sample(q, cache_ckv, cache_kpe, kcat, wuv, l, batch, seq):
    n = batch * seq
    past = cache_ckv.shape[2]
    return pl.pallas_call(
        functools.partial(_attn_sample_kernel, tq=seq),
        grid=(batch,),
        in_specs=[
            pl.BlockSpec((MLA_HEADS, seq, QK_CAT), lambda b: (0, b, 0)),
            pl.BlockSpec((None, None, past, KV_LORA), lambda b: (l, b, 0, 0)),
            pl.BlockSpec((None, None, past, QK_ROPE), lambda b: (l, b, 0, 0)),
            pl.BlockSpec((seq, QK_CAT), lambda b: (b, 0)),
            pl.BlockSpec((None,) + wuv.shape[1:], lambda b: (l, 0, 0, 0)),
        ],
        out_specs=pl.BlockSpec((seq, MLA_HEADS * V_HEAD), lambda b: (b, 0)),
        out_shape=jax.ShapeDtypeStruct((n, MLA_HEADS * V_HEAD), BF16),
        compiler_params=_params("parallel"),
        name="attn_sample",
    )(q, cache_ckv, cache_kpe, kcat, wuv)


def _hgrn_lower_bound(lg, layer, depth):
    rows = [lg[r:r + 1] for r in range(depth)]
    mx = functools.reduce(jnp.maximum, rows)
    ex = [jnp.exp(r - mx) for r in rows]
    tot = functools.reduce(lambda a, b: a + b, ex)
    sm = [e / tot for e in ex]
    cum0 = sm[0]
    cum = functools.reduce(lambda a, b: a + b, sm[:layer + 1])
    return cum - cum0


def _hgrn_kernel(*refs, layer, depth, has_state, tt):
    hq_ref, hf_ref, hi_ref, hg_ref, lbl_ref, og_ref = refs[:6]
    refs = refs[6:]
    if has_state:
        s0_ref, refs = refs[0], refs[1:]
    o_ref, sfin_ref, st_ref, qd_ref, kd_ref, dec_ref, k_ref, b_ref, oacc_ref = refs
    t = pl.program_id(1)
    blk = HGRN_BLOCK
    ind = _group_indicator()
    ind_b = ind.astype(BF16)

    @pl.when(t == 0)
    def _():
        if has_state:
            s0 = s0_ref[...].reshape(WIDTH, HGRN_HEAD_DIM)
            tiled = jnp.concatenate([s0] * HGRN_HEADS, axis=1)
            st_ref[...] = jnp.where(ind, tiled, 0.0).T
        else:
            st_ref[...] = jnp.zeros((WIDTH, WIDTH), F32)

    lb = _hgrn_lower_bound(lbl_ref[...], layer, depth)
    zf = hf_ref[...]
    log_sig = jnp.minimum(zf, 0.0) - jnp.log1p(jnp.exp(-jnp.abs(zf)))
    a = jnp.log(lb)
    b = jnp.log1p(-lb) + log_sig
    logf = jnp.maximum(a, b) + jnp.log1p(jnp.exp(-jnp.abs(a - b)))
    k = (1.0 - lb) * jax.nn.sigmoid(-zf)

    r_i = lax.broadcasted_iota(jnp.int32, (tt, tt), 0)
    c_i = lax.broadcasted_iota(jnp.int32, (tt, tt), 1)
    same = (r_i // blk) == (c_i // blk)
    tri = (same & (c_i <= r_i)).astype(BF16)
    ones = same.astype(BF16)
    hi, mid, lo = _split3(logf)
    bloc = _dot(tri, hi) + _dot(tri, mid) + _dot(tri, lo)
    btot = _dot(ones, hi) + _dot(ones, mid) + _dot(ones, lo)
    qd_ref[...] = hq_ref[...] * jnp.exp(bloc)
    kd_ref[...] = k * jnp.exp(btot - bloc)
    dec_ref[...] = jnp.exp(btot)
    k_ref[...] = k
    b_ref[...] = bloc

    row_i = lax.broadcasted_iota(jnp.int32, (blk, WIDTH), 0)

    def body(i, carry):
        rs = pl.ds(pl.multiple_of(i * blk, blk), blk)
        st = st_ref[...]
        v = hi_ref[rs, :]
        o_inter = _dot_nt(qd_ref[rs, :].astype(BF16), st.astype(BF16))
        kv_t = _dot_tn(v.astype(BF16), kd_ref[rs, :].astype(BF16))
        st_ref[...] = st * dec_ref[rs, :][0:1] + jnp.where(ind, kv_t, 0.0)
        q = hq_ref[rs, :]
        kk = k_ref[rs, :]
        bl = b_ref[rs, :]
        parts = []
        for s in range(blk):
            bs = jnp.broadcast_to(bl[s:s + 1], (blk, WIDTH))
            ks = jnp.broadcast_to(kk[s:s + 1], (blk, WIDTH))
            parts.append(q * ks * jnp.exp(jnp.where(row_i >= s, bl - bs, MASKED)))
        terms = jnp.concatenate(parts, axis=0).astype(BF16)
        a_bc = _dot(terms, ind_b)
        o_intra = jnp.zeros((blk, WIDTH), F32)
        for s in range(blk):
            o_intra = o_intra + a_bc[s * blk:(s + 1) * blk] * jnp.broadcast_to(v[s:s + 1], (blk, WIDTH))
        oacc_ref[rs, :] = o_inter + o_intra
        return carry

    lax.fori_loop(0, tt // blk, body, 0)
    o = _group_rms(oacc_ref[...], og_ref[...], ind_b) * _silu(hg_ref[...])
    o_ref[...] = o.astype(o_ref.dtype)

    @pl.when(t == pl.num_programs(1) - 1)
    def _():
        s_full = st_ref[...].T
        for hd in range(HGRN_HEADS):
            sl = slice(hd * GROUP, (hd + 1) * GROUP)
            sfin_ref[hd] = s_full[sl, sl]


def _hgrn(hq, hf, hi, hg, lb_logits, out_g, state, l, batch, seq, tt):
    n = batch * seq
    nt = seq // tt
    depth = lb_logits.shape[0]
    rspec = pl.BlockSpec((tt, WIDTH), lambda b, t: (b * nt + t, 0))
    in_specs = [rspec] * 4 + [
        pl.BlockSpec((depth, WIDTH), lambda b, t: (0, 0)),
        pl.BlockSpec((None, 1, WIDTH), lambda b, t: (l, 0, 0)),
    ]
    args = [hq, hf, hi, hg, lb_logits, out_g]
    sshape = (HGRN_HEADS, GROUP, HGRN_HEAD_DIM)
    if state is not None:
        in_specs.append(pl.BlockSpec((None, None) + sshape, lambda b, t: (l, b, 0, 0, 0)))
        args.append(state)
    return pl.pallas_call(
        functools.partial(_hgrn_kernel, layer=l, depth=depth, has_state=state is not None, tt=tt),
        grid=(batch, nt),
        in_specs=in_specs,
        out_specs=[rspec, pl.BlockSpec((None,) + sshape, lambda b, t: (b, 0, 0, 0))],
        out_shape=[jax.ShapeDtypeStruct((n, WIDTH), BF16),
                   jax.ShapeDtypeStruct((batch,) + sshape, F32)],
        scratch_shapes=[pltpu.VMEM((WIDTH, WIDTH), F32)] + [pltpu.VMEM((tt, WIDTH), F32)] * 6,
        compiler_params=_params("parallel", "arbitrary"),
        name="hgrn",
    )(*args)


def _out_kernel(oa_ref, ob_ref, cu_ref, cv_ref, x_ref, vg_ref, ws_ref, bias_ref, wout_ref, g_ref,
                *out_refs, lc, emit_v):
    y_ref = out_refs[0]
    tm = x_ref.shape[0]
    ind_b = _group_indicator().astype(BF16)
    u = _gelu_tanh(cu_ref[...])
    v = _group_rms(_gelu_tanh(cv_ref[...]), vg_ref[...], ind_b)
    if emit_v:
        out_refs[1][...] = v
    vb = v.astype(BF16)
    r_i = lax.broadcasted_iota(jnp.int32, (lc, CMLP_GROUPS * lc), 0)
    c_i = lax.broadcasted_iota(jnp.int32, (lc, CMLP_GROUPS * lc), 1)
    w = jnp.where((c_i % lc) <= r_i, ws_ref[...], 0.0).astype(BF16)
    lane_g = lax.broadcasted_iota(jnp.int32, (lc, WIDTH), 1) // GROUP
    bias = bias_ref[...]
    mixed = []
    for c in range(tm // lc):
        vc = vb[c * lc:(c + 1) * lc]
        v_exp = jnp.concatenate([jnp.where(lane_g == g, vc, 0) for g in range(CMLP_GROUPS)], axis=0)
        mixed.append(_dot(w, v_exp) + bias)
    o_c = (u * jnp.concatenate(mixed, axis=0)).astype(BF16)
    na = oa_ref.shape[1]
    nb = na + WIDTH
    mix = (_dot(oa_ref[...], wout_ref[:na, :]) + _dot(ob_ref[...], wout_ref[na:nb, :])
           + _dot(o_c, wout_ref[nb:, :]))
    y_ref[...] = x_ref[...] + _rms(mix, g_ref[...])


def _out(oa, ob, cu, cv, x, norm_g, w, l, lc, emit_v, tm):
    n, d = x.shape
    rspec = lambda wd: pl.BlockSpec((tm, wd), lambda i: (i, 0))
    full = lambda a: pl.BlockSpec((None,) + a.shape[1:], lambda i: (l,) + (0,) * (a.ndim - 1))
    out_specs = [rspec(d)]
    out_shape = [jax.ShapeDtypeStruct((n, d), F32)]
    if emit_v:
        out_specs.append(rspec(WIDTH))
        out_shape.append(jax.ShapeDtypeStruct((n, WIDTH), F32))
    return pl.pallas_call(
        functools.partial(_out_kernel, lc=lc, emit_v=emit_v),
        grid=(n // tm,),
        in_specs=[rspec(oa.shape[1]), rspec(WIDTH), rspec(WIDTH), rspec(WIDTH), rspec(d),
                  full(w["v_g"]), full(w["ws"]), full(w["bias"]), full(w["wout"]),
                  pl.BlockSpec((None, None, 1, d), lambda i: (l, 3, 0, 0))],
        out_specs=out_specs,
        out_shape=out_shape,
        compiler_params=_params("parallel"),
        name="out_proj",
    )(oa, ob, cu, cv, x, w["v_g"], w["ws"], w["bias"], w["wout"], norm_g)


def _rot_cols(w):
    half = QK_ROPE // 2
    return jnp.concatenate([-w[..., half:], w[..., :half]], axis=-1)


def _pad_rope(w):
    return jnp.pad(w, [(0, 0)] * (w.ndim - 1) + [(0, ROPE_PAD - QK_ROPE)])


def _prep_weights(w_in, w_out, mla_qa_g, mla_wqb, mla_kva_g, mla_w_uk, mla_w_uv, hgrn_out_g, cmlp_v_g):
    depth = w_in.shape[0]
    o_kpe = Q_LORA + KV_LORA
    w_kpe = w_in[:, :, o_kpe:o_kpe + QK_ROPE]
    win = jnp.concatenate([w_in[:, :, :o_kpe], w_in[:, :, o_kpe + QK_ROPE:],
                           _pad_rope(w_kpe), _pad_rope(_rot_cols(w_kpe))], axis=-1).astype(BF16)
    wqb = mla_wqb.reshape(depth, Q_LORA, MLA_HEADS, QK_NOPE + QK_ROPE)
    nope = wqb[..., :QK_NOPE].reshape(depth, Q_LORA, MLA_HEADS * QK_NOPE)
    pe = wqb[..., QK_NOPE:]
    flat = lambda a: _pad_rope(a).reshape(depth, Q_LORA, MLA_HEADS * ROPE_PAD)
    wqb = jnp.concatenate([nope, flat(pe), flat(_rot_cols(pe))], axis=-1).astype(BF16)
    return dict(
        win=win, wqb=wqb,
        wuk=jnp.transpose(mla_w_uk, (0, 2, 3, 1)).astype(BF16),
        wuv=jnp.transpose(mla_w_uv, (0, 2, 1, 3)).astype(BF16),
        wout=w_out.astype(BF16),
        qa_g=mla_qa_g[:, None, :], kva_g=mla_kva_g[:, None, :],
        out_g=hgrn_out_g[:, None, :], v_g=cmlp_v_g[:, None, :],
    )


def _rope_tables(pos, rows):
    inv_freq = ROPE_BASE ** (-jnp.arange(0, QK_ROPE, 2, dtype=F32) / QK_ROPE)
    ang = pos.astype(F32)[:, None] * inv_freq[None, :]
    pad = jnp.zeros((pos.shape[0], ROPE_PAD - QK_ROPE), F32)
    cos = jnp.concatenate([jnp.cos(ang), jnp.cos(ang), pad], axis=-1)
    sin = jnp.concatenate([jnp.sin(ang), jnp.sin(ang), pad], axis=-1)
    reps = max(1, rows // pos.shape[0])
    return jnp.tile(cos, (reps, 1)), jnp.tile(sin, (reps, 1))


def _tile(n, want):
    return want if n % want == 0 else n


def _trunk(x3, pos, cache_ckv, cache_kpe, state, norm_g, ffn_w, w, lb_logits, cmlp_w_s, cmlp_b_s):
    batch, seq, d = x3.shape
    n = batch * seq
    depth = norm_g.shape[0]
    x = x3.reshape(n, d)
    tm = _tile(n, 512)
    cos, sin = _rope_tables(pos, tm)
    lc = min(CMLP_CHUNK, seq)
    ws = jnp.transpose(cmlp_w_s[:, :, :lc, :lc], (0, 2, 1, 3)).reshape(depth, lc, CMLP_GROUPS * lc)
    bias = jnp.repeat(jnp.transpose(cmlp_b_s[:, :, :lc], (0, 2, 1)), GROUP, axis=-1)
    w = dict(w, ws=ws, bias=bias)
    is_sample = cache_ckv is not None
    ckv_rows, kpe_rows, states, v_rows = [], [], [], []
    for l in range(depth):
        x = _ffn(x, norm_g, *ffn_w, l, 0, tm)
        tq = None if is_sample else _tile(seq, 256)
        q, c_new, kpe_new, kcat, hq, hf, hi, hg, cu, cv, *ct = _inproj(x, norm_g, l, w, cos, sin, tm, tq)
        if is_sample:
            o_a = _attn_sample(q, cache_ckv, cache_kpe, kcat, w["wuv"], l, batch, seq)
        else:
            o_a = _attn_prompt(q, kcat, ct[0], w["wuv"], l, batch, seq, tq)
        o_b, s_new = _hgrn(hq, hf, hi, hg, lb_logits, w["out_g"], state, l, batch, seq,
                           _tile(seq, 256))
        outs = _out(o_a, o_b, cu, cv, x, norm_g, w, l, lc, is_sample, tm)
        x = outs[0]
        x = _ffn(x, norm_g, *ffn_w, l, 1, tm)
        ckv_rows.append(c_new.reshape(batch, seq, KV_LORA))
        kpe_rows.append(kpe_new.reshape(batch, seq, QK_ROPE))
        states.append(s_new)
        if is_sample:
            v_rows.append(outs[1].reshape(batch, seq, WIDTH))
    res = [x.reshape(batch, seq, d), jnp.stack(ckv_rows), jnp.stack(kpe_rows), jnp.stack(states)]
    if is_sample:
        res.append(jnp.stack(v_rows))
    return res


def kernel(x_prompt, x_sample, cache_mla_ckv, cache_mla_kpe, state_hgrn, norm_g, ffn_w_gate, ffn_w_up,
           ffn_w_down, w_in, w_out, mla_qa_g, mla_wqb, mla_kva_g, mla_w_uk, mla_w_uv, hgrn_lb_logits,
           hgrn_out_g, cmlp_v_g, cmlp_w_s, cmlp_b_s):
    w = _prep_weights(w_in, w_out, mla_qa_g, mla_wqb, mla_kva_g, mla_w_uk, mla_w_uv, hgrn_out_g,
                      cmlp_v_g)
    ffn_w = (ffn_w_gate.astype(BF16), ffn_w_up.astype(BF16), ffn_w_down.astype(BF16))
    depth = norm_g.shape[0]
    ng = norm_g.reshape(depth, norm_g.shape[1], 1, norm_g.shape[2])
    common = (ng, ffn_w, w, hgrn_lb_logits, cmlp_w_s, cmlp_b_s)
    pos_p = jnp.arange(x_prompt.shape[1], dtype=jnp.int32)
    y_p, ckv_p, kpe_p, hgrn_p = _trunk(x_prompt, pos_p, None, None, None, *common)
    pos_s = cache_mla_ckv.shape[2] + jnp.arange(x_sample.shape[1], dtype=jnp.int32)
    y_s, ckv_s, kpe_s, hgrn_s, v_s = _trunk(x_sample, pos_s, cache_mla_ckv, cache_mla_kpe, state_hgrn,
                                            *common)
    return (y_p, y_s, ckv_p, kpe_p, hgrn_p, ckv_s, kpe_s, hgrn_s, v_s)
```

```python
import functools

import jax
import jax.numpy as jnp
from jax import lax
from jax.experimental import pallas as pl
from jax.experimental.pallas import tpu as pltpu

F32 = jnp.float32
BF16 = jnp.bfloat16

EPS = 1e-6
CHUNK = 64
MLA_HEADS = 4
Q_LORA = 384
KV_LORA = 256
QK_NOPE = 128
QK_ROPE = 64
V_HEAD = 128
MLA_SCALE = (QK_NOPE + QK_ROPE) ** -0.5
ROPE_BASE = 10000.0
HGRN_HEADS = 4
HGRN_HEAD_DIM = 64
GROUP = 64
WIDTH = 256
CMLP_GROUPS = 4
CMLP_CHUNK = 128
LANES = 128
ROPE_PAD = LANES
QK_CAT = KV_LORA + ROPE_PAD
Q_SCALE = MLA_SCALE * 1.4426950408889634
HGRN_BLOCK = 16
HGRN_UNROLL = 4
MASKED = -1e30

VMEM_LIMIT = 56 * 1024 * 1024


def _params(*sem):
    return pltpu.CompilerParams(dimension_semantics=sem, vmem_limit_bytes=VMEM_LIMIT)


def _dot(a, b):
    return jnp.dot(a, b, preferred_element_type=F32)


def _dot_nt(a, b):
    return lax.dot_general(a, b, (((1,), (1,)), ((), ())), preferred_element_type=F32)


def _dot_tn(a, b):
    return lax.dot_general(a, b, (((0,), (0,)), ((), ())), preferred_element_type=F32)


def _rms(x, g):
    return x * lax.rsqrt(jnp.mean(x * x, axis=-1, keepdims=True) + EPS) * g


def _silu(x):
    return x * jax.nn.sigmoid(x)


def _gelu_tanh(x):
    return 0.5 * x * (1.0 + jnp.tanh(0.7978845608028654 * (x + 0.044715 * (x * x * x))))


def _group_indicator():
    r = lax.broadcasted_iota(jnp.int32, (WIDTH, WIDTH), 0) // GROUP
    c = lax.broadcasted_iota(jnp.int32, (WIDTH, WIDTH), 1) // GROUP
    return r == c


def _group_rms(x, g, ind_b):
    ms = _dot((x * x).astype(BF16), ind_b) * (1.0 / GROUP)
    return x * lax.rsqrt(ms + EPS) * g


def _split3(x):
    hi = x.astype(BF16)
    r1 = x - hi.astype(F32)
    mid = r1.astype(BF16)
    lo = (r1 - mid.astype(F32)).astype(BF16)
    return hi, mid, lo


def _ffn_kernel(x_ref, gpre_ref, gpost_ref, wg_ref, wu_ref, wd_ref, o_ref, *, tf):
    x = x_ref[...]
    h = _rms(x, gpre_ref[...]).astype(BF16)
    d_ff = wg_ref.shape[1]
    acc = jnp.zeros(x.shape, F32)
    for c in range(d_ff // tf):
        sl = slice(c * tf, (c + 1) * tf)
        g = _dot(h, wg_ref[:, sl])
        u = _dot(h, wu_ref[:, sl])
        a = (_silu(g) * u).astype(BF16)
        acc = acc + _dot(a, wd_ref[sl, :])
    o_ref[...] = x + 0.5 * _rms(acc, gpost_ref[...])


def _ffn(x, norm_g, wg, wu, wd, l, j, tm):
    n, d = x.shape
    d_ff = wg.shape[-1]
    const = lambda r, c: pl.BlockSpec((None, None, r, c), lambda i: (l, j, 0, 0))
    gain = lambda k: pl.BlockSpec((None, None, 1, d), lambda i: (l, k, 0, 0))
    return pl.pallas_call(
        functools.partial(_ffn_kernel, tf=256),
        grid=(n // tm,),
        in_specs=[pl.BlockSpec((tm, d), lambda i: (i, 0)), gain(4 * j), gain(4 * j + 1),
                  const(d, d_ff), const(d, d_ff), const(d_ff, d)],
        out_specs=pl.BlockSpec((tm, d), lambda i: (i, 0)),
        out_shape=jax.ShapeDtypeStruct((n, d), F32),
        compiler_params=_params("parallel"),
        name="ffn",
    )(x, norm_g, norm_g, wg, wu, wd)


def _inproj_kernel(x_ref, g_ref, win_ref, qag_ref, kvag_ref, wqb_ref, wuk_ref, cos_ref, sin_ref,
                   q_ref, c_ref, kpe_ref, kcat_ref, hq_ref, hf_ref, hi_ref, hg_ref, cu_ref, cv_ref,
                   ct_ref=None):
    h = _rms(x_ref[...], g_ref[...]).astype(BF16)
    z = _dot(h, win_ref[...])
    cos = cos_ref[...]
    sin = sin_ref[...]
    o = Q_LORA
    c_new = _rms(z[:, o:o + KV_LORA], kvag_ref[...])
    c_ref[...] = c_new
    kcat_ref[:, :KV_LORA] = c_new.astype(BF16)
    if ct_ref is not None:
        tk = ct_ref.shape[2]
        for u in range(ct_ref.shape[0]):
            ct_ref[u] = c_new[u * tk:(u + 1) * tk].T.astype(BF16)
    o += KV_LORA
    for ref in (hq_ref, hf_ref, hi_ref, hg_ref, cu_ref, cv_ref):
        ref[...] = z[:, o:o + WIDTH]
        o += WIDTH
    kpe = z[:, o:o + ROPE_PAD] * cos + z[:, o + ROPE_PAD:o + 2 * ROPE_PAD] * sin
    kpe_ref[...] = kpe[:, :QK_ROPE]
    kcat_ref[:, KV_LORA:] = kpe.astype(BF16)
    qn = _rms(z[:, :Q_LORA], qag_ref[...]).astype(BF16)
    qq = _dot(qn, wqb_ref[...])
    pe0 = MLA_HEADS * QK_NOPE
    per0 = pe0 + MLA_HEADS * ROPE_PAD
    for hd in range(MLA_HEADS):
        pe = qq[:, pe0 + hd * ROPE_PAD:pe0 + (hd + 1) * ROPE_PAD]
        per = qq[:, per0 + hd * ROPE_PAD:per0 + (hd + 1) * ROPE_PAD]
        q_pe = (pe * cos + per * sin) * Q_SCALE
        nope = qq[:, hd * QK_NOPE:(hd + 1) * QK_NOPE].astype(BF16)
        q_lat = _dot(nope, wuk_ref[hd]) * Q_SCALE
        if ct_ref is not None:
            q_ref[hd, :KV_LORA, :] = q_lat.T.astype(BF16)
            q_ref[hd, KV_LORA:, :] = q_pe.T.astype(BF16)
        else:
            q_ref[hd, :, :KV_LORA] = q_lat.astype(BF16)
            q_ref[hd, :, KV_LORA:] = q_pe.astype(BF16)


def _inproj(x, norm_g, l, w, cos, sin, tm, tk):
    n, d = x.shape
    nt = cos.shape[0] // tm
    row = lambda wd, dt: jax.ShapeDtypeStruct((n, wd), dt)
    rspec = lambda wd: pl.BlockSpec((tm, wd), lambda i: (i, 0))
    full = lambda a: pl.BlockSpec((None,) + a.shape[1:], lambda i: (l,) + (0,) * (a.ndim - 1))
    out_specs = [pl.BlockSpec((MLA_HEADS, tm, QK_CAT), lambda i: (0, i, 0)),
                 rspec(KV_LORA), rspec(QK_ROPE), rspec(QK_CAT)] + [rspec(WIDTH)] * 6
    out_shape = [jax.ShapeDtypeStruct((MLA_HEADS, n, QK_CAT), BF16),
                 row(KV_LORA, F32), row(QK_ROPE, F32), row(QK_CAT, BF16)] + [row(WIDTH, F32)] * 6
    if tk is not None:
        out_specs[0] = pl.BlockSpec((MLA_HEADS, QK_CAT, tm), lambda i: (0, 0, i))
        out_shape[0] = jax.ShapeDtypeStruct((MLA_HEADS, QK_CAT, n), BF16)
        out_specs.append(pl.BlockSpec((tm // tk, KV_LORA, tk), lambda i: (i, 0, 0)))
        out_shape.append(jax.ShapeDtypeStruct((n // tk, KV_LORA, tk), BF16))
    return pl.pallas_call(
        _inproj_kernel,
        grid=(n // tm,),
        in_specs=[
            rspec(d),
            pl.BlockSpec((None, None, 1, d), lambda i: (l, 2, 0, 0)),
            full(w["win"]), full(w["qa_g"]), full(w["kva_g"]), full(w["wqb"]), full(w["wuk"]),
            pl.BlockSpec((tm, ROPE_PAD), lambda i: (i % nt, 0)),
            pl.BlockSpec((tm, ROPE_PAD), lambda i: (i % nt, 0)),
        ],
        out_specs=out_specs,
        out_shape=out_shape,
        compiler_params=_params("parallel"),
        name="in_proj",
    )(x, norm_g, w["win"], w["qa_g"], w["kva_g"], w["wqb"], w["wuk"], cos, sin)


def _attn_prompt_kernel(q_ref, kcat_ref, ct_ref, wuv_ref, o_ref, m_ref, l_ref, acc_ref, *, tq):
    i = pl.program_id(1)
    heads = range(MLA_HEADS)
    m_ref[...] = jnp.full(m_ref.shape, -jnp.inf, F32)
    l_ref[...] = jnp.zeros(l_ref.shape, F32)
    acc_ref[...] = jnp.zeros(acc_ref.shape, F32)

    def scores(j, hd):
        kcat = kcat_ref[pl.ds(pl.multiple_of(j * tq, tq), tq), :]
        return _dot(kcat, q_ref[hd])

    def softmax_pv(s, j, hd, diagonal):
        if diagonal:
            kch = lax.broadcasted_iota(jnp.int32, (tq, tq), 0) // CHUNK
            qc = lax.broadcasted_iota(jnp.int32, (tq, tq), 1) // CHUNK
            s = jnp.where(kch <= qc, s, -jnp.inf)
        m_prev = m_ref[hd]
        m_new = jnp.maximum(m_prev, jnp.max(s, axis=0, keepdims=True))
        alpha = jnp.exp2(m_prev - m_new)
        p = jnp.exp2(s - m_new)
        l_ref[hd] = alpha * l_ref[hd] + jnp.sum(p, axis=0, keepdims=True)
        acc_ref[hd] = alpha * acc_ref[hd] + _dot(ct_ref[j], p.astype(BF16))
        m_ref[hd] = m_new

    s0 = tuple(scores(0, hd) for hd in heads)

    def body(j, s_cur):
        s_next = []
        for hd in heads:
            s_next.append(scores(j + 1, hd))
            softmax_pv(s_cur[hd], j, hd, False)
        return tuple(s_next)

    s_last = lax.fori_loop(0, i, body, s0)
    for hd in heads:
        softmax_pv(s_last[hd], i, hd, True)
    for hd in heads:
        o_lat_t = (acc_ref[hd] / l_ref[hd]).astype(BF16)
        o_ref[:, hd * V_HEAD:(hd + 1) * V_HEAD] = _dot_tn(o_lat_t, wuv_ref[hd]).astype(o_ref.dtype)


def _attn_prompt(q, kcat, ct, wuv, l, batch, seq, tq):
    n = batch * seq
    nq = seq // tq
    return pl.pallas_call(
        functools.partial(_attn_prompt_kernel, tq=tq),
        grid=(batch, nq),
        in_specs=[
            pl.BlockSpec((MLA_HEADS, QK_CAT, tq), lambda b, i: (0, 0, b * nq + i)),
            pl.BlockSpec((seq, QK_CAT), lambda b, i: (b, 0)),
            pl.BlockSpec((nq, KV_LORA, tq), lambda b, i: (b, 0, 0)),
            pl.BlockSpec((None,) + wuv.shape[1:], lambda b, i: (l, 0, 0, 0)),
        ],
        out_specs=pl.BlockSpec((tq, MLA_HEADS * V_HEAD), lambda b, i: (b * nq + i, 0)),
        out_shape=jax.ShapeDtypeStruct((n, MLA_HEADS * V_HEAD), BF16),
        scratch_shapes=[pltpu.VMEM((MLA_HEADS, 1, tq), F32), pltpu.VMEM((MLA_HEADS, 1, tq), F32),
                        pltpu.VMEM((MLA_HEADS, KV_LORA, tq), F32)],
        compiler_params=_params("parallel", "arbitrary"),
        name="attn_prompt",
    )(q, kcat, ct, wuv)


def _attn_sample_kernel(q_ref, ckv_ref, ckpe_ref, kcat_ref, wuv_ref, o_ref, *, tq):
    rows = MLA_HEADS * tq
    q = q_ref[...].reshape(rows, QK_CAT)
    c_past = ckv_ref[...].astype(BF16)
    p_past = ckpe_ref[...].astype(BF16)
    kcat = kcat_ref[...]
    kc = kcat[:, :KV_LORA]
    s_past = _dot_nt(q[:, :KV_LORA], c_past) + _dot_nt(q[:, KV_LORA:KV_LORA + QK_ROPE], p_past)
    s_new = _dot_nt(q, kcat)
    m = jnp.maximum(jnp.max(s_past, axis=-1, keepdims=True), jnp.max(s_new, axis=-1, keepdims=True))
    e_past = jnp.exp2(s_past - m)
    e_new = jnp.exp2(s_new - m)
    l = jnp.sum(e_past, axis=-1, keepdims=True) + jnp.sum(e_new, axis=-1, keepdims=True)
    acc = _dot(e_past.astype(BF16), c_past) + _dot(e_new.astype(BF16), kc)
    o_lat = (acc / l).astype(BF16)
    for hd in range(MLA_HEADS):
        o_ref[:, hd * V_HEAD:(hd + 1) * V_HEAD] = _dot(
            o_lat[hd * tq:(hd + 1) * tq], wuv_ref[hd]).astype(o_ref.dtype)


def _attn_sample(q, cache_ckv, cache_kpe, kcat, wuv, l, batch, seq):
    n = batch * seq
    past = cache_ckv.shape[2]
    return pl.pallas_call(
        functools.partial(_attn_sample_kernel, tq=seq),
        grid=(batch,),
        in_specs=[
            pl.BlockSpec((MLA_HEADS, seq, QK_CAT), lambda b: (0, b, 0)),
            pl.BlockSpec((None, None, past, KV_LORA), lambda b: (l, b, 0, 0)),
            pl.BlockSpec((None, None, past, QK_ROPE), lambda b: (l, b, 0, 0)),
            pl.BlockSpec((seq, QK_CAT), lambda b: (b, 0)),
            pl.BlockSpec((None,) + wuv.shape[1:], lambda b: (l, 0, 0, 0)),
        ],
        out_specs=pl.BlockSpec((seq, MLA_HEADS * V_HEAD), lambda b: (b, 0)),
        out_shape=jax.ShapeDtypeStruct((n, MLA_HEADS * V_HEAD), BF16),
        compiler_params=_params("parallel"),
        name="attn_sample",
    )(q, cache_ckv, cache_kpe, kcat, wuv)


def _hgrn_lower_bound(lg, layer, depth):
    rows = [lg[r:r + 1] for r in range(depth)]
    mx = functools.reduce(jnp.maximum, rows)
    ex = [jnp.exp(r - mx) for r in rows]
    tot = functools.reduce(lambda a, b: a + b, ex)
    sm = [e / tot for e in ex]
    cum0 = sm[0]
    cum = functools.reduce(lambda a, b: a + b, sm[:layer + 1])
    return cum - cum0


def _hgrn_kernel(*refs, layer, depth, has_state, tt):
    hq_ref, hf_ref, hi_ref, hg_ref, lbl_ref, og_ref = refs[:6]
    refs = refs[6:]
    if has_state:
        s0_ref, refs = refs[0], refs[1:]
    o_ref, sfin_ref, st_ref, qd_ref, kd_ref, dec_ref, k_ref, b_ref, oacc_ref = refs
    t = pl.program_id(1)
    blk = HGRN_BLOCK
    ind = _group_indicator()
    ind_b = ind.astype(BF16)

    @pl.when(t == 0)
    def _():
        if has_state:
            s0 = s0_ref[...].reshape(WIDTH, HGRN_HEAD_DIM)
            tiled = jnp.concatenate([s0] * HGRN_HEADS, axis=1)
            st_ref[...] = jnp.where(ind, tiled, 0.0).T
        else:
            st_ref[...] = jnp.zeros((WIDTH, WIDTH), F32)

    lb = _hgrn_lower_bound(lbl_ref[...], layer, depth)
    zf = hf_ref[...]
    log_sig = jnp.minimum(zf, 0.0) - jnp.log1p(jnp.exp(-jnp.abs(zf)))
    a = jnp.log(lb)
    b = jnp.log1p(-lb) + log_sig
    logf = jnp.maximum(a, b) + jnp.log1p(jnp.exp(-jnp.abs(a - b)))
    k = (1.0 - lb) * jax.nn.sigmoid(-zf)

    r_i = lax.broadcasted_iota(jnp.int32, (tt, tt), 0)
    c_i = lax.broadcasted_iota(jnp.int32, (tt, tt), 1)
    same = (r_i // blk) == (c_i // blk)
    tri = (same & (c_i <= r_i)).astype(BF16)
    ones = same.astype(BF16)
    hi, mid, lo = _split3(logf)
    bloc = _dot(tri, hi) + _dot(tri, mid) + _dot(tri, lo)
    btot = _dot(ones, hi) + _dot(ones, mid) + _dot(ones, lo)
    qd_ref[...] = hq_ref[...] * jnp.exp(bloc)
    kd_ref[...] = k * jnp.exp(btot - bloc)
    dec_ref[...] = jnp.exp(btot)
    k_ref[...] = k
    b_ref[...] = bloc

    row_i = lax.broadcasted_iota(jnp.int32, (blk, WIDTH), 0)

    def body(i, carry):
        rs = pl.ds(pl.multiple_of(i * blk, blk), blk)
        st = st_ref[...]
        v = hi_ref[rs, :]
        o_inter = _dot_nt(qd_ref[rs, :].astype(BF16), st.astype(BF16))
        kv_t = _dot_tn(v.astype(BF16), kd_ref[rs, :].astype(BF16))
        st_ref[...] = st * dec_ref[rs, :][0:1] + jnp.where(ind, kv_t, 0.0)
        q = hq_ref[rs, :]
        kk = k_ref[rs, :]
        bl = b_ref[rs, :]
        parts = []
        for s in range(blk):
            bs = jnp.broadcast_to(bl[s:s + 1], (blk, WIDTH))
            ks = jnp.broadcast_to(kk[s:s + 1], (blk, WIDTH))
            parts.append(q * ks * jnp.exp(jnp.where(row_i >= s, bl - bs, MASKED)))
        terms = jnp.concatenate(parts, axis=0).astype(BF16)
        a_bc = _dot(terms, ind_b)
        o_intra = jnp.zeros((blk, WIDTH), F32)
        for s in range(blk):
            o_intra = o_intra + a_bc[s * blk:(s + 1) * blk] * jnp.broadcast_to(v[s:s + 1], (blk, WIDTH))
        oacc_ref[rs, :] = o_inter + o_intra
        return carry

    lax.fori_loop(0, tt // blk, body, 0, unroll=HGRN_UNROLL)
    o = _group_rms(oacc_ref[...], og_ref[...], ind_b) * _silu(hg_ref[...])
    o_ref[...] = o.astype(o_ref.dtype)

    @pl.when(t == pl.num_programs(1) - 1)
    def _():
        s_full = st_ref[...].T
        for hd in range(HGRN_HEADS):
            sl = slice(hd * GROUP, (hd + 1) * GROUP)
            sfin_ref[hd] = s_full[sl, sl]


def _hgrn(hq, hf, hi, hg, lb_logits, out_g, state, l, batch, seq, tt):
    n = batch * seq
    nt = seq // tt
    depth = lb_logits.shape[0]
    rspec = pl.BlockSpec((tt, WIDTH), lambda b, t: (b * nt + t, 0))
    in_specs = [rspec] * 4 + [
        pl.BlockSpec((depth, WIDTH), lambda b, t: (0, 0)),
        pl.BlockSpec((None, 1, WIDTH), lambda b, t: (l, 0, 0)),
    ]
    args = [hq, hf, hi, hg, lb_logits, out_g]
    sshape = (HGRN_HEADS, GROUP, HGRN_HEAD_DIM)
    if state is not None:
        in_specs.append(pl.BlockSpec((None, None) + sshape, lambda b, t: (l, b, 0, 0, 0)))
        args.append(state)
    return pl.pallas_call(
        functools.partial(_hgrn_kernel, layer=l, depth=depth, has_state=state is not None, tt=tt),
        grid=(batch, nt),
        in_specs=in_specs,
        out_specs=[rspec, pl.BlockSpec((None,) + sshape, lambda b, t: (b, 0, 0, 0))],
        out_shape=[jax.ShapeDtypeStruct((n, WIDTH), BF16),
                   jax.ShapeDtypeStruct((batch,) + sshape, F32)],
        scratch_shapes=[pltpu.VMEM((WIDTH, WIDTH), F32)] + [pltpu.VMEM((tt, WIDTH), F32)] * 6,
        compiler_params=_params("parallel", "arbitrary"),
        name="hgrn",
    )(*args)


def _out_kernel(oa_ref, ob_ref, cu_ref, cv_ref, x_ref, vg_ref, ws_ref, bias_ref, wout_ref, g_ref,
                *out_refs, lc, emit_v):
    y_ref = out_refs[0]
    tm = x_ref.shape[0]
    ind_b = _group_indicator().astype(BF16)
    u = _gelu_tanh(cu_ref[...])
    v = _group_rms(_gelu_tanh(cv_ref[...]), vg_ref[...], ind_b)
    if emit_v:
        out_refs[1][...] = v
    vb = v.astype(BF16)
    r_i = lax.broadcasted_iota(jnp.int32, (lc, CMLP_GROUPS * lc), 0)
    c_i = lax.broadcasted_iota(jnp.int32, (lc, CMLP_GROUPS * lc), 1)
    w = jnp.where((c_i % lc) <= r_i, ws_ref[...], 0.0).astype(BF16)
    lane_g = lax.broadcasted_iota(jnp.int32, (lc, WIDTH), 1) // GROUP
    bias = bias_ref[...]
    mixed = []
    for c in range(tm // lc):
        vc = vb[c * lc:(c + 1) * lc]
        v_exp = jnp.concatenate([jnp.where(lane_g == g, vc, 0) for g in range(CMLP_GROUPS)], axis=0)
        mixed.append(_dot(w, v_exp) + bias)
    o_c = (u * jnp.concatenate(mixed, axis=0)).astype(BF16)
    na = oa_ref.shape[1]
    nb = na + WIDTH
    mix = (_dot(oa_ref[...], wout_ref[:na, :]) + _dot(ob_ref[...], wout_ref[na:nb, :])
           + _dot(o_c, wout_ref[nb:, :]))
    y_ref[...] = x_ref[...] + _rms(mix, g_ref[...])


def _out(oa, ob, cu, cv, x, norm_g, w, l, lc, emit_v, tm):
    n, d = x.shape
    rspec = lambda wd: pl.BlockSpec((tm, wd), lambda i: (i, 0))
    full = lambda a: pl.BlockSpec((None,) + a.shape[1:], lambda i: (l,) + (0,) * (a.ndim - 1))
    out_specs = [rspec(d)]
    out_shape = [jax.ShapeDtypeStruct((n, d), F32)]
    if emit_v:
        out_specs.append(rspec(WIDTH))
        out_shape.append(jax.ShapeDtypeStruct((n, WIDTH), F32))
    return pl.pallas_call(
        functools.partial(_out_kernel, lc=lc, emit_v=emit_v),
        grid=(n // tm,),
        in_specs=[rspec(oa.shape[1]), rspec(WIDTH), rspec(WIDTH), rspec(WIDTH), rspec(d),
                  full(w["v_g"]), full(w["ws"]), full(w["bias"]), full(w["wout"]),
                  pl.BlockSpec((None, None, 1, d), lambda i: (l, 3, 0, 0))],
        out_specs=out_specs,
        out_shape=out_shape,
        compiler_params=_params("parallel"),
        name="out_proj",
    )(oa, ob, cu, cv, x, w["v_g"], w["ws"], w["bias"], w["wout"], norm_g)


def _rot_cols(w):
    half = QK_ROPE // 2
    return jnp.concatenate([-w[..., half:], w[..., :half]], axis=-1)


def _pad_rope(w):
    return jnp.pad(w, [(0, 0)] * (w.ndim - 1) + [(0, ROPE_PAD - QK_ROPE)])


def _prep_weights(w_in, w_out, mla_qa_g, mla_wqb, mla_kva_g, mla_w_uk, mla_w_uv, hgrn_out_g, cmlp_v_g):
    depth = w_in.shape[0]
    o_kpe = Q_LORA + KV_LORA
    w_kpe = w_in[:, :, o_kpe:o_kpe + QK_ROPE]
    win = jnp.concatenate([w_in[:, :, :o_kpe], w_in[:, :, o_kpe + QK_ROPE:],
                           _pad_rope(w_kpe), _pad_rope(_rot_cols(w_kpe))], axis=-1).astype(BF16)
    wqb = mla_wqb.reshape(depth, Q_LORA, MLA_HEADS, QK_NOPE + QK_ROPE)
    nope = wqb[..., :QK_NOPE].reshape(depth, Q_LORA, MLA_HEADS * QK_NOPE)
    pe = wqb[..., QK_NOPE:]
    flat = lambda a: _pad_rope(a).reshape(depth, Q_LORA, MLA_HEADS * ROPE_PAD)
    wqb = jnp.concatenate([nope, flat(pe), flat(_rot_cols(pe))], axis=-1).astype(BF16)
    return dict(
        win=win, wqb=wqb,
        wuk=jnp.transpose(mla_w_uk, (0, 2, 3, 1)).astype(BF16),
        wuv=jnp.transpose(mla_w_uv, (0, 2, 1, 3)).astype(BF16),
        wout=w_out.astype(BF16),
        qa_g=mla_qa_g[:, None, :], kva_g=mla_kva_g[:, None, :],
        out_g=hgrn_out_g[:, None, :], v_g=cmlp_v_g[:, None, :],
    )


def _rope_tables(pos, rows):
    inv_freq = ROPE_BASE ** (-jnp.arange(0, QK_ROPE, 2, dtype=F32) / QK_ROPE)
    ang = pos.astype(F32)[:, None] * inv_freq[None, :]
    pad = jnp.zeros((pos.shape[0], ROPE_PAD - QK_ROPE), F32)
    cos = jnp.concatenate([jnp.cos(ang), jnp.cos(ang), pad], axis=-1)
    sin = jnp.concatenate([jnp.sin(ang), jnp.sin(ang), pad], axis=-1)
    reps = max(1, rows // pos.shape[0])
    return jnp.tile(cos, (reps, 1)), jnp.tile(sin, (reps, 1))


def _tile(n, want):
    return want if n % want == 0 else n


def _trunk(x3, pos, cache_ckv, cache_kpe, state, norm_g, ffn_w, w, lb_logits, cmlp_w_s, cmlp_b_s):
    batch, seq, d = x3.shape
    n = batch * seq
    depth = norm_g.shape[0]
    x = x3.reshape(n, d)
    tm = _tile(n, 512)
    cos, sin = _rope_tables(pos, tm)
    lc = min(CMLP_CHUNK, seq)
    ws = jnp.transpose(cmlp_w_s[:, :, :lc, :lc], (0, 2, 1, 3)).reshape(depth, lc, CMLP_GROUPS * lc)
    bias = jnp.repeat(jnp.transpose(cmlp_b_s[:, :, :lc], (0, 2, 1)), GROUP, axis=-1)
    w = dict(w, ws=ws, bias=bias)
    is_sample = cache_ckv is not None
    ckv_rows, kpe_rows, states, v_rows = [], [], [], []
    for l in range(depth):
        x = _ffn(x, norm_g, *ffn_w, l, 0, tm)
        tq = None if is_sample else _tile(seq, 256)
        q, c_new, kpe_new, kcat, hq, hf, hi, hg, cu, cv, *ct = _inproj(x, norm_g, l, w, cos, sin, tm, tq)
        if is_sample:
            o_a = _attn_sample(q, cache_ckv, cache_kpe, kcat, w["wuv"], l, batch, seq)
        else:
            o_a = _attn_prompt(q, kcat, ct[0], w["wuv"], l, batch, seq, tq)
        o_b, s_new = _hgrn(hq, hf, hi, hg, lb_logits, w["out_g"], state, l, batch, seq,
                           _tile(seq, 256))
        outs = _out(o_a, o_b, cu, cv, x, norm_g, w, l, lc, is_sample, tm)
        x = outs[0]
        x = _ffn(x, norm_g, *ffn_w, l, 1, tm)
        ckv_rows.append(c_new.reshape(batch, seq, KV_LORA))
        kpe_rows.append(kpe_new.reshape(batch, seq, QK_ROPE))
        states.append(s_new)
        if is_sample:
            v_rows.append(outs[1].reshape(batch, seq, WIDTH))
    res = [x.reshape(batch, seq, d), jnp.stack(ckv_rows), jnp.stack(kpe_rows), jnp.stack(states)]
    if is_sample:
        res.append(jnp.stack(v_rows))
    return res


def kernel(x_prompt, x_sample, cache_mla_ckv, cache_mla_kpe, state_hgrn, norm_g, ffn_w_gate, ffn_w_up,
           ffn_w_down, w_in, w_out, mla_qa_g, mla_wqb, mla_kva_g, mla_w_uk, mla_w_uv, hgrn_lb_logits,
           hgrn_out_g, cmlp_v_g, cmlp_w_s, cmlp_b_s):
    w = _prep_weights(w_in, w_out, mla_qa_g, mla_wqb, mla_kva_g, mla_w_uk, mla_w_uv, hgrn_out_g,
                      cmlp_v_g)
    ffn_w = (ffn_w_gate.astype(BF16), ffn_w_up.astype(BF16), ffn_w_down.astype(BF16))
    depth = norm_g.shape[0]
    ng = norm_g.reshape(depth, norm_g.shape[1], 1, norm_g.shape[2])
    common = (ng, ffn_w, w, hgrn_lb_logits, cmlp_w_s, cmlp_b_s)
    pos_p = jnp.arange(x_prompt.shape[1], dtype=jnp.int32)
    y_p, ckv_p, kpe_p, hgrn_p = _trunk(x_prompt, pos_p, None, None, None, *common)
    pos_s = cache_mla_ckv.shape[2] + jnp.arange(x_sample.shape[1], dtype=jnp.int32)
    y_s, ckv_s, kpe_s, hgrn_s, v_s = _trunk(x_sample, pos_s, cache_mla_ckv, cache_mla_kpe, state_hgrn,
                                            *common)
    return (y_p, y_s, ckv_p, kpe_p, hgrn_p, ckv_s, kpe_s, hgrn_s, v_s)
```

```python
import functools

import jax
import jax.numpy as jnp
from jax import lax
from jax.experimental import pallas as pl
from jax.experimental.pallas import tpu as pltpu

F32 = jnp.float32
BF16 = jnp.bfloat16

EPS = 1e-6
CHUNK = 64
MLA_HEADS = 4
Q_LORA = 384
KV_LORA = 256
QK_NOPE = 128
QK_ROPE = 64
V_HEAD = 128
MLA_SCALE = (QK_NOPE + QK_ROPE) ** -0.5
ROPE_BASE = 10000.0
HGRN_HEADS = 4
HGRN_HEAD_DIM = 64
GROUP = 64
WIDTH = 256
CMLP_GROUPS = 4
CMLP_CHUNK = 128
LANES = 128
ROPE_PAD = LANES
QK_CAT = KV_LORA + ROPE_PAD
Q_SCALE = MLA_SCALE * 1.4426950408889634
HGRN_BLOCK = 16
HGRN_UNROLL = 8
MASKED = -1e30

VMEM_LIMIT = 56 * 1024 * 1024


def _params(*sem):
    return pltpu.CompilerParams(dimension_semantics=sem, vmem_limit_bytes=VMEM_LIMIT)


def _dot(a, b):
    return jnp.dot(a, b, preferred_element_type=F32)


def _dot_nt(a, b):
    return lax.dot_general(a, b, (((1,), (1,)), ((), ())), preferred_element_type=F32)


def _dot_tn(a, b):
    return lax.dot_general(a, b, (((0,), (0,)), ((), ())), preferred_element_type=F32)


def _rms(x, g):
    return x * lax.rsqrt(jnp.mean(x * x, axis=-1, keepdims=True) + EPS) * g


def _silu(x):
    return x * jax.nn.sigmoid(x)


def _gelu_tanh(x):
    return 0.5 * x * (1.0 + jnp.tanh(0.7978845608028654 * (x + 0.044715 * (x * x * x))))


def _group_indicator():
    r = lax.broadcasted_iota(jnp.int32, (WIDTH, WIDTH), 0) // GROUP
    c = lax.broadcasted_iota(jnp.int32, (WIDTH, WIDTH), 1) // GROUP
    return r == c


def _group_rms(x, g, ind_b):
    ms = _dot((x * x).astype(BF16), ind_b) * (1.0 / GROUP)
    return x * lax.rsqrt(ms + EPS) * g


def _split3(x):
    hi = x.astype(BF16)
    r1 = x - hi.astype(F32)
    mid = r1.astype(BF16)
    lo = (r1 - mid.astype(F32)).astype(BF16)
    return hi, mid, lo


def _ffn_kernel(x_ref, gpre_ref, gpost_ref, wg_ref, wu_ref, wd_ref, o_ref, *, tf):
    x = x_ref[...]
    h = _rms(x, gpre_ref[...]).astype(BF16)
    d_ff = wg_ref.shape[1]
    acc = jnp.zeros(x.shape, F32)
    for c in range(d_ff // tf):
        sl = slice(c * tf, (c + 1) * tf)
        g = _dot(h, wg_ref[:, sl])
        u = _dot(h, wu_ref[:, sl])
        a = (_silu(g) * u).astype(BF16)
        acc = acc + _dot(a, wd_ref[sl, :])
    o_ref[...] = x + 0.5 * _rms(acc, gpost_ref[...])


def _ffn(x, norm_g, wg, wu, wd, l, j, tm):
    n, d = x.shape
    d_ff = wg.shape[-1]
    const = lambda r, c: pl.BlockSpec((None, None, r, c), lambda i: (l, j, 0, 0))
    gain = lambda k: pl.BlockSpec((None, None, 1, d), lambda i: (l, k, 0, 0))
    return pl.pallas_call(
        functools.partial(_ffn_kernel, tf=256),
        grid=(n // tm,),
        in_specs=[pl.BlockSpec((tm, d), lambda i: (i, 0)), gain(4 * j), gain(4 * j + 1),
                  const(d, d_ff), const(d, d_ff), const(d_ff, d)],
        out_specs=pl.BlockSpec((tm, d), lambda i: (i, 0)),
        out_shape=jax.ShapeDtypeStruct((n, d), F32),
        compiler_params=_params("parallel"),
        name="ffn",
    )(x, norm_g, norm_g, wg, wu, wd)


def _inproj_kernel(x_ref, g_ref, win_ref, qag_ref, kvag_ref, wqb_ref, wuk_ref, cos_ref, sin_ref,
                   q_ref, c_ref, kpe_ref, kcat_ref, hq_ref, hf_ref, hi_ref, hg_ref, cu_ref, cv_ref,
                   ct_ref=None):
    h = _rms(x_ref[...], g_ref[...]).astype(BF16)
    z = _dot(h, win_ref[...])
    cos = cos_ref[...]
    sin = sin_ref[...]
    o = Q_LORA
    c_new = _rms(z[:, o:o + KV_LORA], kvag_ref[...])
    c_ref[...] = c_new
    kcat_ref[:, :KV_LORA] = c_new.astype(BF16)
    if ct_ref is not None:
        tk = ct_ref.shape[2]
        for u in range(ct_ref.shape[0]):
            ct_ref[u] = c_new[u * tk:(u + 1) * tk].T.astype(BF16)
    o += KV_LORA
    for ref in (hq_ref, hf_ref, hi_ref, hg_ref, cu_ref, cv_ref):
        ref[...] = z[:, o:o + WIDTH]
        o += WIDTH
    kpe = z[:, o:o + ROPE_PAD] * cos + z[:, o + ROPE_PAD:o + 2 * ROPE_PAD] * sin
    kpe_ref[...] = kpe[:, :QK_ROPE]
    kcat_ref[:, KV_LORA:] = kpe.astype(BF16)
    qn = _rms(z[:, :Q_LORA], qag_ref[...]).astype(BF16)
    qq = _dot(qn, wqb_ref[...])
    pe0 = MLA_HEADS * QK_NOPE
    per0 = pe0 + MLA_HEADS * ROPE_PAD
    for hd in range(MLA_HEADS):
        pe = qq[:, pe0 + hd * ROPE_PAD:pe0 + (hd + 1) * ROPE_PAD]
        per = qq[:, per0 + hd * ROPE_PAD:per0 + (hd + 1) * ROPE_PAD]
        q_pe = (pe * cos + per * sin) * Q_SCALE
        nope = qq[:, hd * QK_NOPE:(hd + 1) * QK_NOPE].astype(BF16)
        q_lat = _dot(nope, wuk_ref[hd]) * Q_SCALE
        if ct_ref is not None:
            q_ref[hd, :KV_LORA, :] = q_lat.T.astype(BF16)
            q_ref[hd, KV_LORA:, :] = q_pe.T.astype(BF16)
        else:
            q_ref[hd, :, :KV_LORA] = q_lat.astype(BF16)
            q_ref[hd, :, KV_LORA:] = q_pe.astype(BF16)


def _inproj(x, norm_g, l, w, cos, sin, tm, tk):
    n, d = x.shape
    nt = cos.shape[0] // tm
    row = lambda wd, dt: jax.ShapeDtypeStruct((n, wd), dt)
    rspec = lambda wd: pl.BlockSpec((tm, wd), lambda i: (i, 0))
    full = lambda a: pl.BlockSpec((None,) + a.shape[1:], lambda i: (l,) + (0,) * (a.ndim - 1))
    out_specs = [pl.BlockSpec((MLA_HEADS, tm, QK_CAT), lambda i: (0, i, 0)),
                 rspec(KV_LORA), rspec(QK_ROPE), rspec(QK_CAT)] + [rspec(WIDTH)] * 6
    out_shape = [jax.ShapeDtypeStruct((MLA_HEADS, n, QK_CAT), BF16),
                 row(KV_LORA, F32), row(QK_ROPE, F32), row(QK_CAT, BF16)] + [row(WIDTH, F32)] * 6
    if tk is not None:
        out_specs[0] = pl.BlockSpec((MLA_HEADS, QK_CAT, tm), lambda i: (0, 0, i))
        out_shape[0] = jax.ShapeDtypeStruct((MLA_HEADS, QK_CAT, n), BF16)
        out_specs.append(pl.BlockSpec((tm // tk, KV_LORA, tk), lambda i: (i, 0, 0)))
        out_shape.append(jax.ShapeDtypeStruct((n // tk, KV_LORA, tk), BF16))
    return pl.pallas_call(
        _inproj_kernel,
        grid=(n // tm,),
        in_specs=[
            rspec(d),
            pl.BlockSpec((None, None, 1, d), lambda i: (l, 2, 0, 0)),
            full(w["win"]), full(w["qa_g"]), full(w["kva_g"]), full(w["wqb"]), full(w["wuk"]),
            pl.BlockSpec((tm, ROPE_PAD), lambda i: (i % nt, 0)),
            pl.BlockSpec((tm, ROPE_PAD), lambda i: (i % nt, 0)),
        ],
        out_specs=out_specs,
        out_shape=out_shape,
        compiler_params=_params("parallel"),
        name="in_proj",
    )(x, norm_g, w["win"], w["qa_g"], w["kva_g"], w["wqb"], w["wuk"], cos, sin)


def _attn_prompt_kernel(q_ref, kcat_ref, ct_ref, wuv_ref, o_ref, m_ref, l_ref, acc_ref, *, tq):
    i = pl.program_id(1)
    heads = range(MLA_HEADS)
    m_ref[...] = jnp.full(m_ref.shape, -jnp.inf, F32)
    l_ref[...] = jnp.zeros(l_ref.shape, F32)
    acc_ref[...] = jnp.zeros(acc_ref.shape, F32)

    def scores(j, hd):
        kcat = kcat_ref[pl.ds(pl.multiple_of(j * tq, tq), tq), :]
        return _dot(kcat, q_ref[hd])

    def softmax_pv(s, j, hd, diagonal):
        if diagonal:
            kch = lax.broadcasted_iota(jnp.int32, (tq, tq), 0) // CHUNK
            qc = lax.broadcasted_iota(jnp.int32, (tq, tq), 1) // CHUNK
            s = jnp.where(kch <= qc, s, -jnp.inf)
        m_prev = m_ref[hd]
        m_new = jnp.maximum(m_prev, jnp.max(s, axis=0, keepdims=True))
        alpha = jnp.exp2(m_prev - m_new)
        p = jnp.exp2(s - m_new)
        l_ref[hd] = alpha * l_ref[hd] + jnp.sum(p, axis=0, keepdims=True)
        acc_ref[hd] = alpha * acc_ref[hd] + _dot(ct_ref[j], p.astype(BF16))
        m_ref[hd] = m_new

    s0 = tuple(scores(0, hd) for hd in heads)

    def body(j, s_cur):
        s_next = []
        for hd in heads:
            s_next.append(scores(j + 1, hd))
            softmax_pv(s_cur[hd], j, hd, False)
        return tuple(s_next)

    odd = i % 2
    s_odd = lax.fori_loop(0, odd, body, s0)
    s_last = lax.fori_loop(0, i // 2, lambda t, s: body(odd + 2 * t + 1, body(odd + 2 * t, s)), s_odd)
    for hd in heads:
        softmax_pv(s_last[hd], i, hd, True)
    for hd in heads:
        o_lat_t = (acc_ref[hd] / l_ref[hd]).astype(BF16)
        o_ref[:, hd * V_HEAD:(hd + 1) * V_HEAD] = _dot_tn(o_lat_t, wuv_ref[hd]).astype(o_ref.dtype)


def _attn_prompt(q, kcat, ct, wuv, l, batch, seq, tq):
    n = batch * seq
    nq = seq // tq
    return pl.pallas_call(
        functools.partial(_attn_prompt_kernel, tq=tq),
        grid=(batch, nq),
        in_specs=[
            pl.BlockSpec((MLA_HEADS, QK_CAT, tq), lambda b, i: (0, 0, b * nq + i)),
            pl.BlockSpec((seq, QK_CAT), lambda b, i: (b, 0)),
            pl.BlockSpec((nq, KV_LORA, tq), lambda b, i: (b, 0, 0)),
            pl.BlockSpec((None,) + wuv.shape[1:], lambda b, i: (l, 0, 0, 0)),
        ],
        out_specs=pl.BlockSpec((tq, MLA_HEADS * V_HEAD), lambda b, i: (b * nq + i, 0)),
        out_shape=jax.ShapeDtypeStruct((n, MLA_HEADS * V_HEAD), BF16),
        scratch_shapes=[pltpu.VMEM((MLA_HEADS, 1, tq), F32), pltpu.VMEM((MLA_HEADS, 1, tq), F32),
                        pltpu.VMEM((MLA_HEADS, KV_LORA, tq), F32)],
        compiler_params=_params("parallel", "arbitrary"),
        name="attn_prompt",
    )(q, kcat, ct, wuv)


def _attn_sample_kernel(q_ref, ckv_ref, ckpe_ref, kcat_ref, wuv_ref, o_ref, *, tq):
    rows = MLA_HEADS * tq
    q = q_ref[...].reshape(rows, QK_CAT)
    c_past = ckv_ref[...].astype(BF16)
    p_past = ckpe_ref[...].astype(BF16)
    kcat = kcat_ref[...]
    kc = kcat[:, :KV_LORA]
    s_past = _dot_nt(q[:, :KV_LORA], c_past) + _dot_nt(q[:, KV_LORA:KV_LORA + QK_ROPE], p_past)
    s_new = _dot_nt(q, kcat)
    m = jnp.maximum(jnp.max(s_past, axis=-1, keepdims=True), jnp.max(s_new, axis=-1, keepdims=True))
    e_past = jnp.exp2(s_past - m)
    e_new = jnp.exp2(s_new - m)
    l = jnp.sum(e_past, axis=-1, keepdims=True) + jnp.sum(e_new, axis=-1, keepdims=True)
    acc = _dot(e_past.astype(BF16), c_past) + _dot(e_new.astype(BF16), kc)
    o_lat = (acc / l).astype(BF16)
    for hd in range(MLA_HEADS):
        o_ref[:, hd * V_HEAD:(hd + 1) * V_HEAD] = _dot(
            o_lat[hd * tq:(hd + 1) * tq], wuv_ref[hd]).astype(o_ref.dtype)


def _attn_sample(q, cache_ckv, cache_kpe, kcat, wuv, l, batch, seq):
    n = batch * seq
    past = cache_ckv.shape[2]
    return pl.pallas_call(
        functools.partial(_attn_sample_kernel, tq=seq),
        grid=(batch,),
        in_specs=[
            pl.BlockSpec((MLA_HEADS, seq, QK_CAT), lambda b: (0, b, 0)),
            pl.BlockSpec((None, None, past, KV_LORA), lambda b: (l, b, 0, 0)),
            pl.BlockSpec((None, None, past, QK_ROPE), lambda b: (l, b, 0, 0)),
            pl.BlockSpec((seq, QK_CAT), lambda b: (b, 0)),
            pl.BlockSpec((None,) + wuv.shape[1:], lambda b: (l, 0, 0, 0)),
        ],
        out_specs=pl.BlockSpec((seq, MLA_HEADS * V_HEAD), lambda b: (b, 0)),
        out_shape=jax.ShapeDtypeStruct((n, MLA_HEADS * V_HEAD), BF16),
        compiler_params=_params("parallel"),
        name="attn_sample",
    )(q, cache_ckv, cache_kpe, kcat, wuv)


def _hgrn_lower_bound(lg, layer, depth):
    rows = [lg[r:r + 1] for r in range(depth)]
    mx = functools.reduce(jnp.maximum, rows)
    ex = [jnp.exp(r - mx) for r in rows]
    tot = functools.reduce(lambda a, b: a + b, ex)
    sm = [e / tot for e in ex]
    cum0 = sm[0]
    cum = functools.reduce(lambda a, b: a + b, sm[:layer + 1])
    return cum - cum0


def _hgrn_kernel(*refs, layer, depth, has_state, tt):
    hq_ref, hf_ref, hi_ref, hg_ref, lbl_ref, og_ref = refs[:6]
    refs = refs[6:]
    if has_state:
        s0_ref, refs = refs[0], refs[1:]
    o_ref, sfin_ref, st_ref, qd_ref, kd_ref, dec_ref, k_ref, b_ref, oacc_ref = refs
    t = pl.program_id(1)
    blk = HGRN_BLOCK
    ind = _group_indicator()
    ind_b = ind.astype(BF16)

    @pl.when(t == 0)
    def _():
        if has_state:
            s0 = s0_ref[...].reshape(WIDTH, HGRN_HEAD_DIM)
            tiled = jnp.concatenate([s0] * HGRN_HEADS, axis=1)
            st_ref[...] = jnp.where(ind, tiled, 0.0).T
        else:
            st_ref[...] = jnp.zeros((WIDTH, WIDTH), F32)

    lb = _hgrn_lower_bound(lbl_ref[...], layer, depth)
    zf = hf_ref[...]
    log_sig = jnp.minimum(zf, 0.0) - jnp.log1p(jnp.exp(-jnp.abs(zf)))
    a = jnp.log(lb)
    b = jnp.log1p(-lb) + log_sig
    logf = jnp.maximum(a, b) + jnp.log1p(jnp.exp(-jnp.abs(a - b)))
    k = (1.0 - lb) * jax.nn.sigmoid(-zf)

    r_i = lax.broadcasted_iota(jnp.int32, (tt, tt), 0)
    c_i = lax.broadcasted_iota(jnp.int32, (tt, tt), 1)
    same = (r_i // blk) == (c_i // blk)
    tri = (same & (c_i <= r_i)).astype(BF16)
    ones = same.astype(BF16)
    hi, mid, lo = _split3(logf)
    bloc = _dot(tri, hi) + _dot(tri, mid) + _dot(tri, lo)
    btot = _dot(ones, hi) + _dot(ones, mid) + _dot(ones, lo)
    qd_ref[...] = hq_ref[...] * jnp.exp(bloc)
    kd_ref[...] = k * jnp.exp(btot - bloc)
    dec_ref[...] = jnp.exp(btot)
    k_ref[...] = k
    b_ref[...] = bloc

    row_i = lax.broadcasted_iota(jnp.int32, (blk, WIDTH), 0)

    def body(i, carry):
        rs = pl.ds(pl.multiple_of(i * blk, blk), blk)
        st = st_ref[...]
        v = hi_ref[rs, :]
        o_inter = _dot_nt(qd_ref[rs, :].astype(BF16), st.astype(BF16))
        kv_t = _dot_tn(v.astype(BF16), kd_ref[rs, :].astype(BF16))
        st_ref[...] = st * dec_ref[rs, :][0:1] + jnp.where(ind, kv_t, 0.0)
        q = hq_ref[rs, :]
        kk = k_ref[rs, :]
        bl = b_ref[rs, :]
        parts = []
        for s in range(blk):
            bs = jnp.broadcast_to(bl[s:s + 1], (blk, WIDTH))
            ks = jnp.broadcast_to(kk[s:s + 1], (blk, WIDTH))
            parts.append(q * ks * jnp.exp(jnp.where(row_i >= s, bl - bs, MASKED)))
        terms = jnp.concatenate(parts, axis=0).astype(BF16)
        a_bc = _dot(terms, ind_b)
        o_intra = jnp.zeros((blk, WIDTH), F32)
        for s in range(blk):
            o_intra = o_intra + a_bc[s * blk:(s + 1) * blk] * jnp.broadcast_to(v[s:s + 1], (blk, WIDTH))
        oacc_ref[rs, :] = o_inter + o_intra
        return carry

    lax.fori_loop(0, tt // blk, body, 0, unroll=min(HGRN_UNROLL, tt // blk))
    o = _group_rms(oacc_ref[...], og_ref[...], ind_b) * _silu(hg_ref[...])
    o_ref[...] = o.astype(o_ref.dtype)

    @pl.when(t == pl.num_programs(1) - 1)
    def _():
        s_full = st_ref[...].T
        for hd in range(HGRN_HEADS):
            sl = slice(hd * GROUP, (hd + 1) * GROUP)
            sfin_ref[hd] = s_full[sl, sl]


def _hgrn(hq, hf, hi, hg, lb_logits, out_g, state, l, batch, seq, tt):
    n = batch * seq
    nt = seq // tt
    depth = lb_logits.shape[0]
    rspec = pl.BlockSpec((tt, WIDTH), lambda b, t: (b * nt + t, 0))
    in_specs = [rspec] * 4 + [
        pl.BlockSpec((depth, WIDTH), lambda b, t: (0, 0)),
        pl.BlockSpec((None, 1, WIDTH), lambda b, t: (l, 0, 0)),
    ]
    args = [hq, hf, hi, hg, lb_logits, out_g]
    sshape = (HGRN_HEADS, GROUP, HGRN_HEAD_DIM)
    if state is not None:
        in_specs.append(pl.BlockSpec((None, None) + sshape, lambda b, t: (l, b, 0, 0, 0)))
        args.append(state)
    return pl.pallas_call(
        functools.partial(_hgrn_kernel, layer=l, depth=depth, has_state=state is not None, tt=tt),
        grid=(batch, nt),
        in_specs=in_specs,
        out_specs=[rspec, pl.BlockSpec((None,) + sshape, lambda b, t: (b, 0, 0, 0))],
        out_shape=[jax.ShapeDtypeStruct((n, WIDTH), BF16),
                   jax.ShapeDtypeStruct((batch,) + sshape, F32)],
        scratch_shapes=[pltpu.VMEM((WIDTH, WIDTH), F32)] + [pltpu.VMEM((tt, WIDTH), F32)] * 6,
        compiler_params=_params("parallel", "arbitrary"),
        name="hgrn",
    )(*args)


def _out_kernel(oa_ref, ob_ref, cu_ref, cv_ref, x_ref, vg_ref, ws_ref, bias_ref, wout_ref, g_ref,
                *out_refs, lc, emit_v):
    y_ref = out_refs[0]
    tm = x_ref.shape[0]
    ind_b = _group_indicator().astype(BF16)
    u = _gelu_tanh(cu_ref[...])
    v = _group_rms(_gelu_tanh(cv_ref[...]), vg_ref[...], ind_b)
    if emit_v:
        out_refs[1][...] = v
    vb = v.astype(BF16)
    r_i = lax.broadcasted_iota(jnp.int32, (lc, CMLP_GROUPS * lc), 0)
    c_i = lax.broadcasted_iota(jnp.int32, (lc, CMLP_GROUPS * lc), 1)
    w = jnp.where((c_i % lc) <= r_i, ws_ref[...], 0.0).astype(BF16)
    lane_g = lax.broadcasted_iota(jnp.int32, (lc, WIDTH), 1) // GROUP
    bias = bias_ref[...]
    mixed = []
    for c in range(tm // lc):
        vc = vb[c * lc:(c + 1) * lc]
        v_exp = jnp.concatenate([jnp.where(lane_g == g, vc, 0) for g in range(CMLP_GROUPS)], axis=0)
        mixed.append(_dot(w, v_exp) + bias)
    o_c = (u * jnp.concatenate(mixed, axis=0)).astype(BF16)
    na = oa_ref.shape[1]
    nb = na + WIDTH
    mix = (_dot(oa_ref[...], wout_ref[:na, :]) + _dot(ob_ref[...], wout_ref[na:nb, :])
           + _dot(o_c, wout_ref[nb:, :]))
    y_ref[...] = x_ref[...] + _rms(mix, g_ref[...])


def _out(oa, ob, cu, cv, x, norm_g, w, l, lc, emit_v, tm):
    n, d = x.shape
    rspec = lambda wd: pl.BlockSpec((tm, wd), lambda i: (i, 0))
    full = lambda a: pl.BlockSpec((None,) + a.shape[1:], lambda i: (l,) + (0,) * (a.ndim - 1))
    out_specs = [rspec(d)]
    out_shape = [jax.ShapeDtypeStruct((n, d), F32)]
    if emit_v:
        out_specs.append(rspec(WIDTH))
        out_shape.append(jax.ShapeDtypeStruct((n, WIDTH), F32))
    return pl.pallas_call(
        functools.partial(_out_kernel, lc=lc, emit_v=emit_v),
        grid=(n // tm,),
        in_specs=[rspec(oa.shape[1]), rspec(WIDTH), rspec(WIDTH), rspec(WIDTH), rspec(d),
                  full(w["v_g"]), full(w["ws"]), full(w["bias"]), full(w["wout"]),
                  pl.BlockSpec((None, None, 1, d), lambda i: (l, 3, 0, 0))],
        out_specs=out_specs,
        out_shape=out_shape,
        compiler_params=_params("parallel"),
        name="out_proj",
    )(oa, ob, cu, cv, x, w["v_g"], w["ws"], w["bias"], w["wout"], norm_g)


def _rot_cols(w):
    half = QK_ROPE // 2
    return jnp.concatenate([-w[..., half:], w[..., :half]], axis=-1)


def _pad_rope(w):
    return jnp.pad(w, [(0, 0)] * (w.ndim - 1) + [(0, ROPE_PAD - QK_ROPE)])


def _prep_weights(w_in, w_out, mla_qa_g, mla_wqb, mla_kva_g, mla_w_uk, mla_w_uv, hgrn_out_g, cmlp_v_g):
    depth = w_in.shape[0]
    o_kpe = Q_LORA + KV_LORA
    w_kpe = w_in[:, :, o_kpe:o_kpe + QK_ROPE]
    win = jnp.concatenate([w_in[:, :, :o_kpe], w_in[:, :, o_kpe + QK_ROPE:],
                           _pad_rope(w_kpe), _pad_rope(_rot_cols(w_kpe))], axis=-1).astype(BF16)
    wqb = mla_wqb.reshape(depth, Q_LORA, MLA_HEADS, QK_NOPE + QK_ROPE)
    nope = wqb[..., :QK_NOPE].reshape(depth, Q_LORA, MLA_HEADS * QK_NOPE)
    pe = wqb[..., QK_NOPE:]
    flat = lambda a: _pad_rope(a).reshape(depth, Q_LORA, MLA_HEADS * ROPE_PAD)
    wqb = jnp.concatenate([nope, flat(pe), flat(_rot_cols(pe))], axis=-1).astype(BF16)
    return dict(
        win=win, wqb=wqb,
        wuk=jnp.transpose(mla_w_uk, (0, 2, 3, 1)).astype(BF16),
        wuv=jnp.transpose(mla_w_uv, (0, 2, 1, 3)).astype(BF16),
        wout=w_out.astype(BF16),
        qa_g=mla_qa_g[:, None, :], kva_g=mla_kva_g[:, None, :],
        out_g=hgrn_out_g[:, None, :], v_g=cmlp_v_g[:, None, :],
    )


def _rope_tables(pos, rows):
    inv_freq = ROPE_BASE ** (-jnp.arange(0, QK_ROPE, 2, dtype=F32) / QK_ROPE)
    ang = pos.astype(F32)[:, None] * inv_freq[None, :]
    pad = jnp.zeros((pos.shape[0], ROPE_PAD - QK_ROPE), F32)
    cos = jnp.concatenate([jnp.cos(ang), jnp.cos(ang), pad], axis=-1)
    sin = jnp.concatenate([jnp.sin(ang), jnp.sin(ang), pad], axis=-1)
    reps = max(1, rows // pos.shape[0])
    return jnp.tile(cos, (reps, 1)), jnp.tile(sin, (reps, 1))


def _tile(n, want):
    return want if n % want == 0 else n


def _trunk(x3, pos, cache_ckv, cache_kpe, state, norm_g, ffn_w, w, lb_logits, cmlp_w_s, cmlp_b_s):
    batch, seq, d = x3.shape
    n = batch * seq
    depth = norm_g.shape[0]
    x = x3.reshape(n, d)
    tm = _tile(n, 512)
    cos, sin = _rope_tables(pos, tm)
    lc = min(CMLP_CHUNK, seq)
    ws = jnp.transpose(cmlp_w_s[:, :, :lc, :lc], (0, 2, 1, 3)).reshape(depth, lc, CMLP_GROUPS * lc)
    bias = jnp.repeat(jnp.transpose(cmlp_b_s[:, :, :lc], (0, 2, 1)), GROUP, axis=-1)
    w = dict(w, ws=ws, bias=bias)
    is_sample = cache_ckv is not None
    ckv_rows, kpe_rows, states, v_rows = [], [], [], []
    for l in range(depth):
        x = _ffn(x, norm_g, *ffn_w, l, 0, tm)
        tq = None if is_sample else _tile(seq, 256)
        q, c_new, kpe_new, kcat, hq, hf, hi, hg, cu, cv, *ct = _inproj(x, norm_g, l, w, cos, sin, tm, tq)
        if is_sample:
            o_a = _attn_sample(q, cache_ckv, cache_kpe, kcat, w["wuv"], l, batch, seq)
        else:
            o_a = _attn_prompt(q, kcat, ct[0], w["wuv"], l, batch, seq, tq)
        o_b, s_new = _hgrn(hq, hf, hi, hg, lb_logits, w["out_g"], state, l, batch, seq,
                           _tile(seq, 256))
        outs = _out(o_a, o_b, cu, cv, x, norm_g, w, l, lc, is_sample, tm)
        x = outs[0]
        x = _ffn(x, norm_g, *ffn_w, l, 1, tm)
        ckv_rows.append(c_new.reshape(batch, seq, KV_LORA))
        kpe_rows.append(kpe_new.reshape(batch, seq, QK_ROPE))
        states.append(s_new)
        if is_sample:
            v_rows.append(outs[1].reshape(batch, seq, WIDTH))
    res = [x.reshape(batch, seq, d), jnp.stack(ckv_rows), jnp.stack(kpe_rows), jnp.stack(states)]
    if is_sample:
        res.append(jnp.stack(v_rows))
    return res


def kernel(x_prompt, x_sample, cache_mla_ckv, cache_mla_kpe, state_hgrn, norm_g, ffn_w_gate, ffn_w_up,
           ffn_w_down, w_in, w_out, mla_qa_g, mla_wqb, mla_kva_g, mla_w_uk, mla_w_uv, hgrn_lb_logits,
           hgrn_out_g, cmlp_v_g, cmlp_w_s, cmlp_b_s):
    w = _prep_weights(w_in, w_out, mla_qa_g, mla_wqb, mla_kva_g, mla_w_uk, mla_w_uv, hgrn_out_g,
                      cmlp_v_g)
    ffn_w = (ffn_w_gate.astype(BF16), ffn_w_up.astype(BF16), ffn_w_down.astype(BF16))
    depth = norm_g.shape[0]
    ng = norm_g.reshape(depth, norm_g.shape[1], 1, norm_g.shape[2])
    common = (ng, ffn_w, w, hgrn_lb_logits, cmlp_w_s, cmlp_b_s)
    pos_p = jnp.arange(x_prompt.shape[1], dtype=jnp.int32)
    y_p, ckv_p, kpe_p, hgrn_p = _trunk(x_prompt, pos_p, None, None, None, *common)
    pos_s = cache_mla_ckv.shape[2] + jnp.arange(x_sample.shape[1], dtype=jnp.int32)
    y_s, ckv_s, kpe_s, hgrn_s, v_s = _trunk(x_sample, pos_s, cache_mla_ckv, cache_mla_kpe, state_hgrn,
                                            *common)
    return (y_p, y_s, ckv_p, kpe_p, hgrn_p, ckv_s, kpe_s, hgrn_s, v_s)
```

```python
import functools

import jax
import jax.numpy as jnp
from jax import lax
from jax.experimental import pallas as pl
from jax.experimental.pallas import tpu as pltpu

F32 = jnp.float32
BF16 = jnp.bfloat16

EPS = 1e-6
CHUNK = 64
MLA_HEADS = 4
Q_LORA = 384
KV_LORA = 256
QK_NOPE = 128
QK_ROPE = 64
V_HEAD = 128
MLA_SCALE = (QK_NOPE + QK_ROPE) ** -0.5
ROPE_BASE = 10000.0
HGRN_HEADS = 4
HGRN_HEAD_DIM = 64
GROUP = 64
WIDTH = 256
CMLP_GROUPS = 4
CMLP_CHUNK = 128
LANES = 128
ROPE_PAD = LANES
QK_CAT = KV_LORA + ROPE_PAD
Q_SCALE = MLA_SCALE * 1.4426950408889634
HGRN_BLOCK = 16
HGRN_UNROLL = 16
MASKED = -1e30

VMEM_LIMIT = 56 * 1024 * 1024


def _params(*sem):
    return pltpu.CompilerParams(dimension_semantics=sem, vmem_limit_bytes=VMEM_LIMIT)


def _dot(a, b):
    return jnp.dot(a, b, preferred_element_type=F32)


def _dot_nt(a, b):
    return lax.dot_general(a, b, (((1,), (1,)), ((), ())), preferred_element_type=F32)


def _dot_tn(a, b):
    return lax.dot_general(a, b, (((0,), (0,)), ((), ())), preferred_element_type=F32)


def _rms(x, g):
    return x * lax.rsqrt(jnp.mean(x * x, axis=-1, keepdims=True) + EPS) * g


def _silu(x):
    return x * jax.nn.sigmoid(x)


def _gelu_tanh(x):
    return 0.5 * x * (1.0 + jnp.tanh(0.7978845608028654 * (x + 0.044715 * (x * x * x))))


def _group_indicator():
    r = lax.broadcasted_iota(jnp.int32, (WIDTH, WIDTH), 0) // GROUP
    c = lax.broadcasted_iota(jnp.int32, (WIDTH, WIDTH), 1) // GROUP
    return r == c


def _group_rms(x, g, ind_b):
    ms = _dot((x * x).astype(BF16), ind_b) * (1.0 / GROUP)
    return x * lax.rsqrt(ms + EPS) * g


def _split3(x):
    hi = x.astype(BF16)
    r1 = x - hi.astype(F32)
    mid = r1.astype(BF16)
    lo = (r1 - mid.astype(F32)).astype(BF16)
    return hi, mid, lo


def _ffn_kernel(x_ref, gpre_ref, gpost_ref, wg_ref, wu_ref, wd_ref, o_ref, *, tf):
    x = x_ref[...]
    h = _rms(x, gpre_ref[...]).astype(BF16)
    d_ff = wg_ref.shape[1]
    acc = jnp.zeros(x.shape, F32)
    for c in range(d_ff // tf):
        sl = slice(c * tf, (c + 1) * tf)
        g = _dot(h, wg_ref[:, sl])
        u = _dot(h, wu_ref[:, sl])
        a = (_silu(g) * u).astype(BF16)
        acc = acc + _dot(a, wd_ref[sl, :])
    o_ref[...] = x + 0.5 * _rms(acc, gpost_ref[...])


def _ffn(x, norm_g, wg, wu, wd, l, j, tm):
    n, d = x.shape
    d_ff = wg.shape[-1]
    const = lambda r, c: pl.BlockSpec((None, None, r, c), lambda i: (l, j, 0, 0))
    gain = lambda k: pl.BlockSpec((None, None, 1, d), lambda i: (l, k, 0, 0))
    return pl.pallas_call(
        functools.partial(_ffn_kernel, tf=256),
        grid=(n // tm,),
        in_specs=[pl.BlockSpec((tm, d), lambda i: (i, 0)), gain(4 * j), gain(4 * j + 1),
                  const(d, d_ff), const(d, d_ff), const(d_ff, d)],
        out_specs=pl.BlockSpec((tm, d), lambda i: (i, 0)),
        out_shape=jax.ShapeDtypeStruct((n, d), F32),
        compiler_params=_params("parallel"),
        name="ffn",
    )(x, norm_g, norm_g, wg, wu, wd)


def _inproj_kernel(x_ref, g_ref, win_ref, qag_ref, kvag_ref, wqb_ref, wuk_ref, cos_ref, sin_ref,
                   q_ref, c_ref, kpe_ref, kcat_ref, hq_ref, hf_ref, hi_ref, hg_ref, cu_ref, cv_ref,
                   ct_ref=None):
    h = _rms(x_ref[...], g_ref[...]).astype(BF16)
    z = _dot(h, win_ref[...])
    cos = cos_ref[...]
    sin = sin_ref[...]
    o = Q_LORA
    c_new = _rms(z[:, o:o + KV_LORA], kvag_ref[...])
    c_ref[...] = c_new
    kcat_ref[:, :KV_LORA] = c_new.astype(BF16)
    if ct_ref is not None:
        tk = ct_ref.shape[2]
        for u in range(ct_ref.shape[0]):
            ct_ref[u] = c_new[u * tk:(u + 1) * tk].T.astype(BF16)
    o += KV_LORA
    for ref in (hq_ref, hf_ref, hi_ref, hg_ref, cu_ref, cv_ref):
        ref[...] = z[:, o:o + WIDTH]
        o += WIDTH
    kpe = z[:, o:o + ROPE_PAD] * cos + z[:, o + ROPE_PAD:o + 2 * ROPE_PAD] * sin
    kpe_ref[...] = kpe[:, :QK_ROPE]
    kcat_ref[:, KV_LORA:] = kpe.astype(BF16)
    qn = _rms(z[:, :Q_LORA], qag_ref[...]).astype(BF16)
    qq = _dot(qn, wqb_ref[...])
    pe0 = MLA_HEADS * QK_NOPE
    per0 = pe0 + MLA_HEADS * ROPE_PAD
    for hd in range(MLA_HEADS):
        pe = qq[:, pe0 + hd * ROPE_PAD:pe0 + (hd + 1) * ROPE_PAD]
        per = qq[:, per0 + hd * ROPE_PAD:per0 + (hd + 1) * ROPE_PAD]
        q_pe = (pe * cos + per * sin) * Q_SCALE
        nope = qq[:, hd * QK_NOPE:(hd + 1) * QK_NOPE].astype(BF16)
        q_lat = _dot(nope, wuk_ref[hd]) * Q_SCALE
        if ct_ref is not None:
            q_ref[hd, :KV_LORA, :] = q_lat.T.astype(BF16)
            q_ref[hd, KV_LORA:, :] = q_pe.T.astype(BF16)
        else:
            q_ref[hd, :, :KV_LORA] = q_lat.astype(BF16)
            q_ref[hd, :, KV_LORA:] = q_pe.astype(BF16)


def _inproj(x, norm_g, l, w, cos, sin, tm, tk):
    n, d = x.shape
    nt = cos.shape[0] // tm
    row = lambda wd, dt: jax.ShapeDtypeStruct((n, wd), dt)
    rspec = lambda wd: pl.BlockSpec((tm, wd), lambda i: (i, 0))
    full = lambda a: pl.BlockSpec((None,) + a.shape[1:], lambda i: (l,) + (0,) * (a.ndim - 1))
    out_specs = [pl.BlockSpec((MLA_HEADS, tm, QK_CAT), lambda i: (0, i, 0)),
                 rspec(KV_LORA), rspec(QK_ROPE), rspec(QK_CAT)] + [rspec(WIDTH)] * 6
    out_shape = [jax.ShapeDtypeStruct((MLA_HEADS, n, QK_CAT), BF16),
                 row(KV_LORA, F32), row(QK_ROPE, F32), row(QK_CAT, BF16)] + [row(WIDTH, F32)] * 6
    if tk is not None:
        out_specs[0] = pl.BlockSpec((MLA_HEADS, QK_CAT, tm), lambda i: (0, 0, i))
        out_shape[0] = jax.ShapeDtypeStruct((MLA_HEADS, QK_CAT, n), BF16)
        out_specs.append(pl.BlockSpec((tm // tk, KV_LORA, tk), lambda i: (i, 0, 0)))
        out_shape.append(jax.ShapeDtypeStruct((n // tk, KV_LORA, tk), BF16))
    return pl.pallas_call(
        _inproj_kernel,
        grid=(n // tm,),
        in_specs=[
            rspec(d),
            pl.BlockSpec((None, None, 1, d), lambda i: (l, 2, 0, 0)),
            full(w["win"]), full(w["qa_g"]), full(w["kva_g"]), full(w["wqb"]), full(w["wuk"]),
            pl.BlockSpec((tm, ROPE_PAD), lambda i: (i % nt, 0)),
            pl.BlockSpec((tm, ROPE_PAD), lambda i: (i % nt, 0)),
        ],
        out_specs=out_specs,
        out_shape=out_shape,
        compiler_params=_params("parallel"),
        name="in_proj",
    )(x, norm_g, w["win"], w["qa_g"], w["kva_g"], w["wqb"], w["wuk"], cos, sin)


def _attn_prompt_kernel(q_ref, kcat_ref, ct_ref, wuv_ref, o_ref, m_ref, l_ref, acc_ref, *, tq):
    i = pl.program_id(1)
    heads = range(MLA_HEADS)
    m_ref[...] = jnp.full(m_ref.shape, -jnp.inf, F32)
    l_ref[...] = jnp.zeros(l_ref.shape, F32)
    acc_ref[...] = jnp.zeros(acc_ref.shape, F32)

    def scores(j, hd):
        kcat = kcat_ref[pl.ds(pl.multiple_of(j * tq, tq), tq), :]
        return _dot(kcat, q_ref[hd])

    def softmax_pv(s, j, hd, diagonal):
        if diagonal:
            kch = lax.broadcasted_iota(jnp.int32, (tq, tq), 0) // CHUNK
            qc = lax.broadcasted_iota(jnp.int32, (tq, tq), 1) // CHUNK
            s = jnp.where(kch <= qc, s, -jnp.inf)
        m_prev = m_ref[hd]
        m_new = jnp.maximum(m_prev, jnp.max(s, axis=0, keepdims=True))
        alpha = jnp.exp2(m_prev - m_new)
        p = jnp.exp2(s - m_new)
        l_ref[hd] = alpha * l_ref[hd] + jnp.sum(p, axis=0, keepdims=True)
        acc_ref[hd] = alpha * acc_ref[hd] + _dot(ct_ref[j], p.astype(BF16))
        m_ref[hd] = m_new

    s0 = tuple(scores(0, hd) for hd in heads)

    def body(j, s_cur):
        s_next = []
        for hd in heads:
            s_next.append(scores(j + 1, hd))
            softmax_pv(s_cur[hd], j, hd, False)
        return tuple(s_next)

    def trips(per_trip, first):
        def trip(t, s):
            for u in range(per_trip):
                s = body(first + per_trip * t + u, s)
            return s
        return trip

    r1 = i % 2
    r2 = (i // 2) % 2
    s_last = lax.fori_loop(0, r1, trips(1, 0), s0)
    s_last = lax.fori_loop(0, r2, trips(2, r1), s_last)
    s_last = lax.fori_loop(0, i // 4, trips(4, r1 + 2 * r2), s_last)
    for hd in heads:
        softmax_pv(s_last[hd], i, hd, True)
    for hd in heads:
        o_lat_t = (acc_ref[hd] / l_ref[hd]).astype(BF16)
        o_ref[:, hd * V_HEAD:(hd + 1) * V_HEAD] = _dot_tn(o_lat_t, wuv_ref[hd]).astype(o_ref.dtype)


def _attn_prompt(q, kcat, ct, wuv, l, batch, seq, tq):
    n = batch * seq
    nq = seq // tq
    return pl.pallas_call(
        functools.partial(_attn_prompt_kernel, tq=tq),
        grid=(batch, nq),
        in_specs=[
            pl.BlockSpec((MLA_HEADS, QK_CAT, tq), lambda b, i: (0, 0, b * nq + i)),
            pl.BlockSpec((seq, QK_CAT), lambda b, i: (b, 0)),
            pl.BlockSpec((nq, KV_LORA, tq), lambda b, i: (b, 0, 0)),
            pl.BlockSpec((None,) + wuv.shape[1:], lambda b, i: (l, 0, 0, 0)),
        ],
        out_specs=pl.BlockSpec((tq, MLA_HEADS * V_HEAD), lambda b, i: (b * nq + i, 0)),
        out_shape=jax.ShapeDtypeStruct((n, MLA_HEADS * V_HEAD), BF16),
        scratch_shapes=[pltpu.VMEM((MLA_HEADS, 1, tq), F32), pltpu.VMEM((MLA_HEADS, 1, tq), F32),
                        pltpu.VMEM((MLA_HEADS, KV_LORA, tq), F32)],
        compiler_params=_params("parallel", "arbitrary"),
        name="attn_prompt",
    )(q, kcat, ct, wuv)


def _attn_sample_kernel(q_ref, ckv_ref, ckpe_ref, kcat_ref, wuv_ref, o_ref, *, tq):
    rows = MLA_HEADS * tq
    q = q_ref[...].reshape(rows, QK_CAT)
    c_past = ckv_ref[...].astype(BF16)
    p_past = ckpe_ref[...].astype(BF16)
    kcat = kcat_ref[...]
    kc = kcat[:, :KV_LORA]
    s_past = _dot_nt(q[:, :KV_LORA], c_past) + _dot_nt(q[:, KV_LORA:KV_LORA + QK_ROPE], p_past)
    s_new = _dot_nt(q, kcat)
    m = jnp.maximum(jnp.max(s_past, axis=-1, keepdims=True), jnp.max(s_new, axis=-1, keepdims=True))
    e_past = jnp.exp2(s_past - m)
    e_new = jnp.exp2(s_new - m)
    l = jnp.sum(e_past, axis=-1, keepdims=True) + jnp.sum(e_new, axis=-1, keepdims=True)
    acc = _dot(e_past.astype(BF16), c_past) + _dot(e_new.astype(BF16), kc)
    o_lat = (acc / l).astype(BF16)
    for hd in range(MLA_HEADS):
        o_ref[:, hd * V_HEAD:(hd + 1) * V_HEAD] = _dot(
            o_lat[hd * tq:(hd + 1) * tq], wuv_ref[hd]).astype(o_ref.dtype)


def _attn_sample(q, cache_ckv, cache_kpe, kcat, wuv, l, batch, seq):
    n = batch * seq
    past = cache_ckv.shape[2]
    return pl.pallas_call(
        functools.partial(_attn_sample_kernel, tq=seq),
        grid=(batch,),
        in_specs=[
            pl.BlockSpec((MLA_HEADS, seq, QK_CAT), lambda b: (0, b, 0)),
            pl.BlockSpec((None, None, past, KV_LORA), lambda b: (l, b, 0, 0)),
            pl.BlockSpec((None, None, past, QK_ROPE), lambda b: (l, b, 0, 0)),
            pl.BlockSpec((seq, QK_CAT), lambda b: (b, 0)),
            pl.BlockSpec((None,) + wuv.shape[1:], lambda b: (l, 0, 0, 0)),
        ],
        out_specs=pl.BlockSpec((seq, MLA_HEADS * V_HEAD), lambda b: (b, 0)),
        out_shape=jax.ShapeDtypeStruct((n, MLA_HEADS * V_HEAD), BF16),
        compiler_params=_params("parallel"),
        name="attn_sample",
    )(q, cache_ckv, cache_kpe, kcat, wuv)


def _hgrn_lower_bound(lg, layer, depth):
    rows = [lg[r:r + 1] for r in range(depth)]
    mx = functools.reduce(jnp.maximum, rows)
    ex = [jnp.exp(r - mx) for r in rows]
    tot = functools.reduce(lambda a, b: a + b, ex)
    sm = [e / tot for e in ex]
    cum0 = sm[0]
    cum = functools.reduce(lambda a, b: a + b, sm[:layer + 1])
    return cum - cum0


def _hgrn_kernel(*refs, layer, depth, has_state, tt):
    hq_ref, hf_ref, hi_ref, hg_ref, lbl_ref, og_ref = refs[:6]
    refs = refs[6:]
    if has_state:
        s0_ref, refs = refs[0], refs[1:]
    o_ref, sfin_ref, st_ref, qd_ref, kd_ref, dec_ref, k_ref, b_ref, oacc_ref = refs
    t = pl.program_id(1)
    blk = HGRN_BLOCK
    ind = _group_indicator()
    ind_b = ind.astype(BF16)

    @pl.when(t == 0)
    def _():
        if has_state:
            s0 = s0_ref[...].reshape(WIDTH, HGRN_HEAD_DIM)
            tiled = jnp.concatenate([s0] * HGRN_HEADS, axis=1)
            st_ref[...] = jnp.where(ind, tiled, 0.0).T
        else:
            st_ref[...] = jnp.zeros((WIDTH, WIDTH), F32)

    lb = _hgrn_lower_bound(lbl_ref[...], layer, depth)
    zf = hf_ref[...]
    log_sig = jnp.minimum(zf, 0.0) - jnp.log1p(jnp.exp(-jnp.abs(zf)))
    a = jnp.log(lb)
    b = jnp.log1p(-lb) + log_sig
    logf = jnp.maximum(a, b) + jnp.log1p(jnp.exp(-jnp.abs(a - b)))
    k = (1.0 - lb) * jax.nn.sigmoid(-zf)

    r_i = lax.broadcasted_iota(jnp.int32, (tt, tt), 0)
    c_i = lax.broadcasted_iota(jnp.int32, (tt, tt), 1)
    same = (r_i // blk) == (c_i // blk)
    tri = (same & (c_i <= r_i)).astype(BF16)
    ones = same.astype(BF16)
    hi, mid, lo = _split3(logf)
    bloc = _dot(tri, hi) + _dot(tri, mid) + _dot(tri, lo)
    btot = _dot(ones, hi) + _dot(ones, mid) + _dot(ones, lo)
    qd_ref[...] = hq_ref[...] * jnp.exp(bloc)
    kd_ref[...] = k * jnp.exp(btot - bloc)
    dec_ref[...] = jnp.exp(btot)
    k_ref[...] = k
    b_ref[...] = bloc

    row_i = lax.broadcasted_iota(jnp.int32, (blk, WIDTH), 0)

    def body(i, carry):
        rs = pl.ds(pl.multiple_of(i * blk, blk), blk)
        st = st_ref[...]
        v = hi_ref[rs, :]
        o_inter = _dot_nt(qd_ref[rs, :].astype(BF16), st.astype(BF16))
        kv_t = _dot_tn(v.astype(BF16), kd_ref[rs, :].astype(BF16))
        dec = dec_ref[rs, :][0:1]
        for lo in (0, WIDTH // 2):
            sl = slice(lo, lo + WIDTH // 2)
            st_ref[sl, sl] = st[sl, sl] * dec[:, sl] + jnp.where(ind[sl, sl], kv_t[sl, sl], 0.0)
        q = hq_ref[rs, :]
        kk = k_ref[rs, :]
        bl = b_ref[rs, :]
        parts = []
        for s in range(blk):
            bs = jnp.broadcast_to(bl[s:s + 1], (blk, WIDTH))
            ks = jnp.broadcast_to(kk[s:s + 1], (blk, WIDTH))
            parts.append(q * ks * jnp.exp(jnp.where(row_i >= s, bl - bs, MASKED)))
        terms = jnp.concatenate(parts, axis=0).astype(BF16)
        a_bc = _dot(terms, ind_b)
        o_intra = jnp.zeros((blk, WIDTH), F32)
        for s in range(blk):
            o_intra = o_intra + a_bc[s * blk:(s + 1) * blk] * jnp.broadcast_to(v[s:s + 1], (blk, WIDTH))
        oacc_ref[rs, :] = o_inter + o_intra
        return carry

    lax.fori_loop(0, tt // blk, body, 0, unroll=min(HGRN_UNROLL, tt // blk))
    o = _group_rms(oacc_ref[...], og_ref[...], ind_b) * _silu(hg_ref[...])
    o_ref[...] = o.astype(o_ref.dtype)

    @pl.when(t == pl.num_programs(1) - 1)
    def _():
        s_full = st_ref[...].T
        for hd in range(HGRN_HEADS):
            sl = slice(hd * GROUP, (hd + 1) * GROUP)
            sfin_ref[hd] = s_full[sl, sl]


def _hgrn(hq, hf, hi, hg, lb_logits, out_g, state, l, batch, seq, tt):
    n = batch * seq
    nt = seq // tt
    depth = lb_logits.shape[0]
    rspec = pl.BlockSpec((tt, WIDTH), lambda b, t: (b * nt + t, 0))
    in_specs = [rspec] * 4 + [
        pl.BlockSpec((depth, WIDTH), lambda b, t: (0, 0)),
        pl.BlockSpec((None, 1, WIDTH), lambda b, t: (l, 0, 0)),
    ]
    args = [hq, hf, hi, hg, lb_logits, out_g]
    sshape = (HGRN_HEADS, GROUP, HGRN_HEAD_DIM)
    if state is not None:
        in_specs.append(pl.BlockSpec((None, None) + sshape, lambda b, t: (l, b, 0, 0, 0)))
        args.append(state)
    return pl.pallas_call(
        functools.partial(_hgrn_kernel, layer=l, depth=depth, has_state=state is not None, tt=tt),
        grid=(batch, nt),
        in_specs=in_specs,
        out_specs=[rspec, pl.BlockSpec((None,) + sshape, lambda b, t: (b, 0, 0, 0))],
        out_shape=[jax.ShapeDtypeStruct((n, WIDTH), BF16),
                   jax.ShapeDtypeStruct((batch,) + sshape, F32)],
        scratch_shapes=[pltpu.VMEM((WIDTH, WIDTH), F32)] + [pltpu.VMEM((tt, WIDTH), F32)] * 6,
        compiler_params=_params("parallel", "arbitrary"),
        name="hgrn",
    )(*args)


def _out_kernel(oa_ref, ob_ref, cu_ref, cv_ref, x_ref, vg_ref, ws_ref, bias_ref, wout_ref, g_ref,
                *out_refs, lc, emit_v):
    y_ref = out_refs[0]
    tm = x_ref.shape[0]
    ind_b = _group_indicator().astype(BF16)
    u = _gelu_tanh(cu_ref[...])
    v = _group_rms(_gelu_tanh(cv_ref[...]), vg_ref[...], ind_b)
    if emit_v:
        out_refs[1][...] = v
    vb = v.astype(BF16)
    r_i = lax.broadcasted_iota(jnp.int32, (lc, CMLP_GROUPS * lc), 0)
    c_i = lax.broadcasted_iota(jnp.int32, (lc, CMLP_GROUPS * lc), 1)
    w = jnp.where((c_i % lc) <= r_i, ws_ref[...], 0.0).astype(BF16)
    lane_g = lax.broadcasted_iota(jnp.int32, (lc, WIDTH), 1) // GROUP
    bias = bias_ref[...]
    mixed = []
    for c in range(tm // lc):
        vc = vb[c * lc:(c + 1) * lc]
        v_exp = jnp.concatenate([jnp.where(lane_g == g, vc, 0) for g in range(CMLP_GROUPS)], axis=0)
        mixed.append(_dot(w, v_exp) + bias)
    o_c = (u * jnp.concatenate(mixed, axis=0)).astype(BF16)
    na = oa_ref.shape[1]
    nb = na + WIDTH
    mix = (_dot(oa_ref[...], wout_ref[:na, :]) + _dot(ob_ref[...], wout_ref[na:nb, :])
           + _dot(o_c, wout_ref[nb:, :]))
    y_ref[...] = x_ref[...] + _rms(mix, g_ref[...])


def _out(oa, ob, cu, cv, x, norm_g, w, l, lc, emit_v, tm):
    n, d = x.shape
    rspec = lambda wd: pl.BlockSpec((tm, wd), lambda i: (i, 0))
    full = lambda a: pl.BlockSpec((None,) + a.shape[1:], lambda i: (l,) + (0,) * (a.ndim - 1))
    out_specs = [rspec(d)]
    out_shape = [jax.ShapeDtypeStruct((n, d), F32)]
    if emit_v:
        out_specs.append(rspec(WIDTH))
        out_shape.append(jax.ShapeDtypeStruct((n, WIDTH), F32))
    return pl.pallas_call(
        functools.partial(_out_kernel, lc=lc, emit_v=emit_v),
        grid=(n // tm,),
        in_specs=[rspec(oa.shape[1]), rspec(WIDTH), rspec(WIDTH), rspec(WIDTH), rspec(d),
                  full(w["v_g"]), full(w["ws"]), full(w["bias"]), full(w["wout"]),
                  pl.BlockSpec((None, None, 1, d), lambda i: (l, 3, 0, 0))],
        out_specs=out_specs,
        out_shape=out_shape,
        compiler_params=_params("parallel"),
        name="out_proj",
    )(oa, ob, cu, cv, x, w["v_g"], w["ws"], w["bias"], w["wout"], norm_g)


def _rot_cols(w):
    half = QK_ROPE // 2
    return jnp.concatenate([-w[..., half:], w[..., :half]], axis=-1)


def _pad_rope(w):
    return jnp.pad(w, [(0, 0)] * (w.ndim - 1) + [(0, ROPE_PAD - QK_ROPE)])


def _prep_weights(w_in, w_out, mla_qa_g, mla_wqb, mla_kva_g, mla_w_uk, mla_w_uv, hgrn_out_g, cmlp_v_g):
    depth = w_in.shape[0]
    o_kpe = Q_LORA + KV_LORA
    w_kpe = w_in[:, :, o_kpe:o_kpe + QK_ROPE]
    win = jnp.concatenate([w_in[:, :, :o_kpe], w_in[:, :, o_kpe + QK_ROPE:],
                           _pad_rope(w_kpe), _pad_rope(_rot_cols(w_kpe))], axis=-1).astype(BF16)
    wqb = mla_wqb.reshape(depth, Q_LORA, MLA_HEADS, QK_NOPE + QK_ROPE)
    nope = wqb[..., :QK_NOPE].reshape(depth, Q_LORA, MLA_HEADS * QK_NOPE)
    pe = wqb[..., QK_NOPE:]
    flat = lambda a: _pad_rope(a).reshape(depth, Q_LORA, MLA_HEADS * ROPE_PAD)
    wqb = jnp.concatenate([nope, flat(pe), flat(_rot_cols(pe))], axis=-1).astype(BF16)
    return dict(
        win=win, wqb=wqb,
        wuk=jnp.transpose(mla_w_uk, (0, 2, 3, 1)).astype(BF16),
        wuv=jnp.transpose(mla_w_uv, (0, 2, 1, 3)).astype(BF16),
        wout=w_out.astype(BF16),
        qa_g=mla_qa_g[:, None, :], kva_g=mla_kva_g[:, None, :],
        out_g=hgrn_out_g[:, None, :], v_g=cmlp_v_g[:, None, :],
    )


def _rope_tables(pos, rows):
    inv_freq = ROPE_BASE ** (-jnp.arange(0, QK_ROPE, 2, dtype=F32) / QK_ROPE)
    ang = pos.astype(F32)[:, None] * inv_freq[None, :]
    pad = jnp.zeros((pos.shape[0], ROPE_PAD - QK_ROPE), F32)
    cos = jnp.concatenate([jnp.cos(ang), jnp.cos(ang), pad], axis=-1)
    sin = jnp.concatenate([jnp.sin(ang), jnp.sin(ang), pad], axis=-1)
    reps = max(1, rows // pos.shape[0])
    return jnp.tile(cos, (reps, 1)), jnp.tile(sin, (reps, 1))


def _tile(n, want):
    return want if n % want == 0 else n


def _trunk(x3, pos, cache_ckv, cache_kpe, state, norm_g, ffn_w, w, lb_logits, cmlp_w_s, cmlp_b_s):
    batch, seq, d = x3.shape
    n = batch * seq
    depth = norm_g.shape[0]
    x = x3.reshape(n, d)
    tm = _tile(n, 512)
    cos, sin = _rope_tables(pos, tm)
    lc = min(CMLP_CHUNK, seq)
    ws = jnp.transpose(cmlp_w_s[:, :, :lc, :lc], (0, 2, 1, 3)).reshape(depth, lc, CMLP_GROUPS * lc)
    bias = jnp.repeat(jnp.transpose(cmlp_b_s[:, :, :lc], (0, 2, 1)), GROUP, axis=-1)
    w = dict(w, ws=ws, bias=bias)
    is_sample = cache_ckv is not None
    ckv_rows, kpe_rows, states, v_rows = [], [], [], []
    for l in range(depth):
        x = _ffn(x, norm_g, *ffn_w, l, 0, tm)
        tq = None if is_sample else _tile(seq, 256)
        q, c_new, kpe_new, kcat, hq, hf, hi, hg, cu, cv, *ct = _inproj(x, norm_g, l, w, cos, sin, tm, tq)
        if is_sample:
            o_a = _attn_sample(q, cache_ckv, cache_kpe, kcat, w["wuv"], l, batch, seq)
        else:
            o_a = _attn_prompt(q, kcat, ct[0], w["wuv"], l, batch, seq, tq)
        o_b, s_new = _hgrn(hq, hf, hi, hg, lb_logits, w["out_g"], state, l, batch, seq,
                           _tile(seq, 256))
        outs = _out(o_a, o_b, cu, cv, x, norm_g, w, l, lc, is_sample, tm)
        x = outs[0]
        x = _ffn(x, norm_g, *ffn_w, l, 1, tm)
        ckv_rows.append(c_new.reshape(batch, seq, KV_LORA))
        kpe_rows.append(kpe_new.reshape(batch, seq, QK_ROPE))
        states.append(s_new)
        if is_sample:
            v_rows.append(outs[1].reshape(batch, seq, WIDTH))
    res = [x.reshape(batch, seq, d), jnp.stack(ckv_rows), jnp.stack(kpe_rows), jnp.stack(states)]
    if is_sample:
        res.append(jnp.stack(v_rows))
    return res


def kernel(x_prompt, x_sample, cache_mla_ckv, cache_mla_kpe, state_hgrn, norm_g, ffn_w_gate, ffn_w_up,
           ffn_w_down, w_in, w_out, mla_qa_g, mla_wqb, mla_kva_g, mla_w_uk, mla_w_uv, hgrn_lb_logits,
           hgrn_out_g, cmlp_v_g, cmlp_w_s, cmlp_b_s):
    w = _prep_weights(w_in, w_out, mla_qa_g, mla_wqb, mla_kva_g, mla_w_uk, mla_w_uv, hgrn_out_g,
                      cmlp_v_g)
    ffn_w = (ffn_w_gate.astype(BF16), ffn_w_up.astype(BF16), ffn_w_down.astype(BF16))
    depth = norm_g.shape[0]
    ng = norm_g.reshape(depth, norm_g.shape[1], 1, norm_g.shape[2])
    common = (ng, ffn_w, w, hgrn_lb_logits, cmlp_w_s, cmlp_b_s)
    pos_p = jnp.arange(x_prompt.shape[1], dtype=jnp.int32)
    y_p, ckv_p, kpe_p, hgrn_p = _trunk(x_prompt, pos_p, None, None, None, *common)
    pos_s = cache_mla_ckv.shape[2] + jnp.arange(x_sample.shape[1], dtype=jnp.int32)
    y_s, ckv_s, kpe_s, hgrn_s, v_s = _trunk(x_sample, pos_s, cache_mla_ckv, cache_mla_kpe, state_hgrn,
                                            *common)
    return (y_p, y_s, ckv_p, kpe_p, hgrn_p, ckv_s, kpe_s, hgrn_s, v_s)
```

```python
import functools

import jax
import jax.numpy as jnp
from jax import lax
from jax.experimental import pallas as pl
from jax.experimental.pallas import tpu as pltpu

F32 = jnp.float32
BF16 = jnp.bfloat16

EPS = 1e-6
CHUNK = 64
MLA_HEADS = 4
Q_LORA = 384
KV_LORA = 256
QK_NOPE = 128
QK_ROPE = 64
V_HEAD = 128
MLA_SCALE = (QK_NOPE + QK_ROPE) ** -0.5
ROPE_BASE = 10000.0
HGRN_HEADS = 4
HGRN_HEAD_DIM = 64
GROUP = 64
WIDTH = 256
CMLP_GROUPS = 4
CMLP_CHUNK = 128
LANES = 128
ROPE_PAD = LANES
QK_CAT = KV_LORA + ROPE_PAD
Q_SCALE = MLA_SCALE * 1.4426950408889634
HGRN_BLOCK = 16
HGRN_UNROLL = 16
MASKED = -1e30

VMEM_LIMIT = 56 * 1024 * 1024


def _params(*sem):
    return pltpu.CompilerParams(dimension_semantics=sem, vmem_limit_bytes=VMEM_LIMIT)


def _dot(a, b):
    return jnp.dot(a, b, preferred_element_type=F32)


def _dot_nt(a, b):
    return lax.dot_general(a, b, (((1,), (1,)), ((), ())), preferred_element_type=F32)


def _dot_tn(a, b):
    return lax.dot_general(a, b, (((0,), (0,)), ((), ())), preferred_element_type=F32)


def _rms(x, g):
    return x * lax.rsqrt(jnp.mean(x * x, axis=-1, keepdims=True) + EPS) * g


def _silu(x):
    return x * jax.nn.sigmoid(x)


def _gelu_tanh(x):
    return 0.5 * x * (1.0 + jnp.tanh(0.7978845608028654 * (x + 0.044715 * (x * x * x))))


def _group_indicator():
    r = lax.broadcasted_iota(jnp.int32, (WIDTH, WIDTH), 0) // GROUP
    c = lax.broadcasted_iota(jnp.int32, (WIDTH, WIDTH), 1) // GROUP
    return r == c


def _group_rms(x, g, ind_b):
    ms = _dot((x * x).astype(BF16), ind_b) * (1.0 / GROUP)
    return x * lax.rsqrt(ms + EPS) * g


def _split3(x):
    hi = x.astype(BF16)
    r1 = x - hi.astype(F32)
    mid = r1.astype(BF16)
    lo = (r1 - mid.astype(F32)).astype(BF16)
    return hi, mid, lo


def _ffn_kernel(x_ref, gpre_ref, gpost_ref, wg_ref, wu_ref, wd_ref, o_ref, *, tf):
    x = x_ref[...]
    h = _rms(x, gpre_ref[...]).astype(BF16)
    d_ff = wg_ref.shape[1]
    acc = jnp.zeros(x.shape, F32)
    for c in range(d_ff // tf):
        sl = slice(c * tf, (c + 1) * tf)
        g = _dot(h, wg_ref[:, sl])
        u = _dot(h, wu_ref[:, sl])
        a = (_silu(g) * u).astype(BF16)
        acc = acc + _dot(a, wd_ref[sl, :])
    o_ref[...] = x + 0.5 * _rms(acc, gpost_ref[...])


def _ffn(x, norm_g, wg, wu, wd, l, j, tm):
    n, d = x.shape
    d_ff = wg.shape[-1]
    const = lambda r, c: pl.BlockSpec((None, None, r, c), lambda i: (l, j, 0, 0))
    gain = lambda k: pl.BlockSpec((None, None, 1, d), lambda i: (l, k, 0, 0))
    return pl.pallas_call(
        functools.partial(_ffn_kernel, tf=256),
        grid=(n // tm,),
        in_specs=[pl.BlockSpec((tm, d), lambda i: (i, 0)), gain(4 * j), gain(4 * j + 1),
                  const(d, d_ff), const(d, d_ff), const(d_ff, d)],
        out_specs=pl.BlockSpec((tm, d), lambda i: (i, 0)),
        out_shape=jax.ShapeDtypeStruct((n, d), F32),
        compiler_params=_params("parallel"),
        name="ffn",
    )(x, norm_g, norm_g, wg, wu, wd)


def _inproj_kernel(x_ref, g_ref, win_ref, qag_ref, kvag_ref, wqb_ref, wuk_ref, cos_ref, sin_ref,
                   q_ref, c_ref, kpe_ref, kcat_ref, hq_ref, hf_ref, hi_ref, hg_ref, cu_ref, cv_ref,
                   ct_ref=None):
    h = _rms(x_ref[...], g_ref[...]).astype(BF16)
    z = _dot(h, win_ref[...])
    cos = cos_ref[...]
    sin = sin_ref[...]
    o = Q_LORA
    c_new = _rms(z[:, o:o + KV_LORA], kvag_ref[...])
    c_ref[...] = c_new
    kcat_ref[:, :KV_LORA] = c_new.astype(BF16)
    if ct_ref is not None:
        tk = ct_ref.shape[2]
        for u in range(ct_ref.shape[0]):
            ct_ref[u] = c_new[u * tk:(u + 1) * tk].T.astype(BF16)
    o += KV_LORA
    for ref in (hq_ref, hf_ref, hi_ref, hg_ref, cu_ref, cv_ref):
        ref[...] = z[:, o:o + WIDTH]
        o += WIDTH
    kpe = z[:, o:o + ROPE_PAD] * cos + z[:, o + ROPE_PAD:o + 2 * ROPE_PAD] * sin
    kpe_ref[...] = kpe[:, :QK_ROPE]
    kcat_ref[:, KV_LORA:] = kpe.astype(BF16)
    qn = _rms(z[:, :Q_LORA], qag_ref[...]).astype(BF16)
    qq = _dot(qn, wqb_ref[...])
    pe0 = MLA_HEADS * QK_NOPE
    per0 = pe0 + MLA_HEADS * ROPE_PAD
    for hd in range(MLA_HEADS):
        pe = qq[:, pe0 + hd * ROPE_PAD:pe0 + (hd + 1) * ROPE_PAD]
        per = qq[:, per0 + hd * ROPE_PAD:per0 + (hd + 1) * ROPE_PAD]
        q_pe = (pe * cos + per * sin) * Q_SCALE
        nope = qq[:, hd * QK_NOPE:(hd + 1) * QK_NOPE].astype(BF16)
        q_lat = _dot(nope, wuk_ref[hd]) * Q_SCALE
        if ct_ref is not None:
            q_ref[hd, :KV_LORA, :] = q_lat.T.astype(BF16)
            q_ref[hd, KV_LORA:, :] = q_pe.T.astype(BF16)
        else:
            q_ref[hd, :, :KV_LORA] = q_lat.astype(BF16)
            q_ref[hd, :, KV_LORA:] = q_pe.astype(BF16)


def _inproj(x, norm_g, l, w, cos, sin, tm, tk):
    n, d = x.shape
    nt = cos.shape[0] // tm
    row = lambda wd, dt: jax.ShapeDtypeStruct((n, wd), dt)
    rspec = lambda wd: pl.BlockSpec((tm, wd), lambda i: (i, 0))
    full = lambda a: pl.BlockSpec((None,) + a.shape[1:], lambda i: (l,) + (0,) * (a.ndim - 1))
    out_specs = [pl.BlockSpec((MLA_HEADS, tm, QK_CAT), lambda i: (0, i, 0)),
                 rspec(KV_LORA), rspec(QK_ROPE), rspec(QK_CAT)] + [rspec(WIDTH)] * 6
    out_shape = [jax.ShapeDtypeStruct((MLA_HEADS, n, QK_CAT), BF16),
                 row(KV_LORA, F32), row(QK_ROPE, F32), row(QK_CAT, BF16)] + [row(WIDTH, F32)] * 6
    if tk is not None:
        out_specs[0] = pl.BlockSpec((MLA_HEADS, QK_CAT, tm), lambda i: (0, 0, i))
        out_shape[0] = jax.ShapeDtypeStruct((MLA_HEADS, QK_CAT, n), BF16)
        out_specs.append(pl.BlockSpec((tm // tk, KV_LORA, tk), lambda i: (i, 0, 0)))
        out_shape.append(jax.ShapeDtypeStruct((n // tk, KV_LORA, tk), BF16))
    return pl.pallas_call(
        _inproj_kernel,
        grid=(n // tm,),
        in_specs=[
            rspec(d),
            pl.BlockSpec((None, None, 1, d), lambda i: (l, 2, 0, 0)),
            full(w["win"]), full(w["qa_g"]), full(w["kva_g"]), full(w["wqb"]), full(w["wuk"]),
            pl.BlockSpec((tm, ROPE_PAD), lambda i: (i % nt, 0)),
            pl.BlockSpec((tm, ROPE_PAD), lambda i: (i % nt, 0)),
        ],
        out_specs=out_specs,
        out_shape=out_shape,
        compiler_params=_params("parallel"),
        name="in_proj",
    )(x, norm_g, w["win"], w["qa_g"], w["kva_g"], w["wqb"], w["wuk"], cos, sin)


def _attn_prompt_kernel(q_ref, kcat_ref, ct_ref, wuv_ref, o_ref, m_ref, l_ref, acc_ref, *, tq):
    i = pl.program_id(1)
    heads = range(MLA_HEADS)
    m_ref[...] = jnp.full(m_ref.shape, -jnp.inf, F32)
    l_ref[...] = jnp.zeros(l_ref.shape, F32)
    acc_ref[...] = jnp.zeros(acc_ref.shape, F32)

    def scores(j, hd):
        kcat = kcat_ref[pl.ds(pl.multiple_of(j * tq, tq), tq), :]
        return _dot(kcat, q_ref[hd])

    def softmax_pv(s, j, hd, diagonal):
        if diagonal:
            kch = lax.broadcasted_iota(jnp.int32, (tq, tq), 0) // CHUNK
            qc = lax.broadcasted_iota(jnp.int32, (tq, tq), 1) // CHUNK
            s = jnp.where(kch <= qc, s, -jnp.inf)
        m_prev = m_ref[hd]
        m_new = jnp.maximum(m_prev, jnp.max(s, axis=0, keepdims=True))
        alpha = jnp.exp2(m_prev - m_new)
        p = jnp.exp2(s - m_new)
        l_ref[hd] = alpha * l_ref[hd] + jnp.sum(p, axis=0, keepdims=True)
        acc_ref[hd] = alpha * acc_ref[hd] + _dot(ct_ref[j], p.astype(BF16))
        m_ref[hd] = m_new

    s0 = tuple(scores(0, hd) for hd in heads)

    def body(j, s_cur):
        s_next = []
        for hd in heads:
            s_next.append(scores(j + 1, hd))
            softmax_pv(s_cur[hd], j, hd, False)
        return tuple(s_next)

    def trips(per_trip, first):
        def trip(t, s):
            for u in range(per_trip):
                s = body(first + per_trip * t + u, s)
            return s
        return trip

    r1 = i % 2
    r2 = (i // 2) % 2
    s_last = lax.fori_loop(0, r1, trips(1, 0), s0)
    s_last = lax.fori_loop(0, r2, trips(2, r1), s_last)
    s_last = lax.fori_loop(0, i // 4, trips(4, r1 + 2 * r2), s_last)
    for hd in heads:
        softmax_pv(s_last[hd], i, hd, True)
    for hd in heads:
        o_lat_t = (acc_ref[hd] / l_ref[hd]).astype(BF16)
        o_ref[:, hd * V_HEAD:(hd + 1) * V_HEAD] = _dot_tn(o_lat_t, wuv_ref[hd]).astype(o_ref.dtype)


def _attn_prompt(q, kcat, ct, wuv, l, batch, seq, tq):
    n = batch * seq
    nq = seq // tq
    return pl.pallas_call(
        functools.partial(_attn_prompt_kernel, tq=tq),
        grid=(batch, nq),
        in_specs=[
            pl.BlockSpec((MLA_HEADS, QK_CAT, tq), lambda b, i: (0, 0, b * nq + i)),
            pl.BlockSpec((seq, QK_CAT), lambda b, i: (b, 0)),
            pl.BlockSpec((nq, KV_LORA, tq), lambda b, i: (b, 0, 0)),
            pl.BlockSpec((None,) + wuv.shape[1:], lambda b, i: (l, 0, 0, 0)),
        ],
        out_specs=pl.BlockSpec((tq, MLA_HEADS * V_HEAD), lambda b, i: (b * nq + i, 0)),
        out_shape=jax.ShapeDtypeStruct((n, MLA_HEADS * V_HEAD), BF16),
        scratch_shapes=[pltpu.VMEM((MLA_HEADS, 1, tq), F32), pltpu.VMEM((MLA_HEADS, 1, tq), F32),
                        pltpu.VMEM((MLA_HEADS, KV_LORA, tq), F32)],
        compiler_params=_params("parallel", "arbitrary"),
        name="attn_prompt",
    )(q, kcat, ct, wuv)


def _attn_sample_kernel(q_ref, ckv_ref, ckpe_ref, kcat_ref, wuv_ref, o_ref, *, tq, nb):
    rows = MLA_HEADS * tq

    def scores(u):
        q = q_ref[:, u * tq:(u + 1) * tq, :].reshape(rows, QK_CAT)
        c_past = ckv_ref[u].astype(BF16)
        p_past = ckpe_ref[u].astype(BF16)
        kcat = kcat_ref[u * tq:(u + 1) * tq, :]
        s_past = _dot_nt(q[:, :KV_LORA], c_past) + _dot_nt(q[:, KV_LORA:KV_LORA + QK_ROPE], p_past)
        return s_past, _dot_nt(q, kcat), c_past, kcat[:, :KV_LORA]

    def finish(u, s_past, s_new, c_past, kc):
        m = jnp.maximum(jnp.max(s_past, axis=-1, keepdims=True), jnp.max(s_new, axis=-1, keepdims=True))
        e_past = jnp.exp2(s_past - m)
        e_new = jnp.exp2(s_new - m)
        l = jnp.sum(e_past, axis=-1, keepdims=True) + jnp.sum(e_new, axis=-1, keepdims=True)
        acc = _dot(e_past.astype(BF16), c_past) + _dot(e_new.astype(BF16), kc)
        o_lat = (acc / l).astype(BF16)
        for hd in range(MLA_HEADS):
            o_ref[u * tq:(u + 1) * tq, hd * V_HEAD:(hd + 1) * V_HEAD] = _dot(
                o_lat[hd * tq:(hd + 1) * tq], wuv_ref[hd]).astype(o_ref.dtype)

    staged = [scores(u) for u in range(nb)]
    for u in range(nb):
        finish(u, *staged[u])


def _attn_sample(q, cache_ckv, cache_kpe, kcat, wuv, l, batch, seq):
    n = batch * seq
    past = cache_ckv.shape[2]
    nb = 2 if batch % 2 == 0 else 1
    return pl.pallas_call(
        functools.partial(_attn_sample_kernel, tq=seq, nb=nb),
        grid=(batch // nb,),
        in_specs=[
            pl.BlockSpec((MLA_HEADS, nb * seq, QK_CAT), lambda b: (0, b, 0)),
            pl.BlockSpec((None, nb, past, KV_LORA), lambda b: (l, b, 0, 0)),
            pl.BlockSpec((None, nb, past, QK_ROPE), lambda b: (l, b, 0, 0)),
            pl.BlockSpec((nb * seq, QK_CAT), lambda b: (b, 0)),
            pl.BlockSpec((None,) + wuv.shape[1:], lambda b: (l, 0, 0, 0)),
        ],
        out_specs=pl.BlockSpec((nb * seq, MLA_HEADS * V_HEAD), lambda b: (b, 0)),
        out_shape=jax.ShapeDtypeStruct((n, MLA_HEADS * V_HEAD), BF16),
        compiler_params=_params("parallel"),
        name="attn_sample",
    )(q, cache_ckv, cache_kpe, kcat, wuv)


def _hgrn_lower_bound(lg, layer, depth):
    rows = [lg[r:r + 1] for r in range(depth)]
    mx = functools.reduce(jnp.maximum, rows)
    ex = [jnp.exp(r - mx) for r in rows]
    tot = functools.reduce(lambda a, b: a + b, ex)
    sm = [e / tot for e in ex]
    cum0 = sm[0]
    cum = functools.reduce(lambda a, b: a + b, sm[:layer + 1])
    return cum - cum0


def _hgrn_kernel(*refs, layer, depth, has_state, tt):
    hq_ref, hf_ref, hi_ref, hg_ref, lbl_ref, og_ref = refs[:6]
    refs = refs[6:]
    if has_state:
        s0_ref, refs = refs[0], refs[1:]
    o_ref, sfin_ref, st_ref, qd_ref, kd_ref, dec_ref, k_ref, b_ref, oacc_ref = refs
    t = pl.program_id(1)
    blk = HGRN_BLOCK
    ind = _group_indicator()
    ind_b = ind.astype(BF16)

    @pl.when(t == 0)
    def _():
        if has_state:
            s0 = s0_ref[...].reshape(WIDTH, HGRN_HEAD_DIM)
            tiled = jnp.concatenate([s0] * HGRN_HEADS, axis=1)
            st_ref[...] = jnp.where(ind, tiled, 0.0).T
        else:
            st_ref[...] = jnp.zeros((WIDTH, WIDTH), F32)

    lb = _hgrn_lower_bound(lbl_ref[...], layer, depth)
    zf = hf_ref[...]
    log_sig = jnp.minimum(zf, 0.0) - jnp.log1p(jnp.exp(-jnp.abs(zf)))
    a = jnp.log(lb)
    b = jnp.log1p(-lb) + log_sig
    logf = jnp.maximum(a, b) + jnp.log1p(jnp.exp(-jnp.abs(a - b)))
    k = (1.0 - lb) * jax.nn.sigmoid(-zf)

    r_i = lax.broadcasted_iota(jnp.int32, (tt, tt), 0)
    c_i = lax.broadcasted_iota(jnp.int32, (tt, tt), 1)
    same = (r_i // blk) == (c_i // blk)
    tri = (same & (c_i <= r_i)).astype(BF16)
    ones = same.astype(BF16)
    hi, mid, lo = _split3(logf)
    bloc = _dot(tri, hi) + _dot(tri, mid) + _dot(tri, lo)
    btot = _dot(ones, hi) + _dot(ones, mid) + _dot(ones, lo)
    qd_ref[...] = hq_ref[...] * jnp.exp(bloc)
    kd_ref[...] = k * jnp.exp(btot - bloc)
    dec_ref[...] = jnp.exp(btot)
    k_ref[...] = k
    b_ref[...] = bloc

    row_i = lax.broadcasted_iota(jnp.int32, (blk, WIDTH), 0)

    def body(i, carry):
        rs = pl.ds(pl.multiple_of(i * blk, blk), blk)
        st = st_ref[...]
        v = hi_ref[rs, :]
        o_inter = _dot_nt(qd_ref[rs, :].astype(BF16), st.astype(BF16))
        kv_t = _dot_tn(v.astype(BF16), kd_ref[rs, :].astype(BF16))
        dec = dec_ref[rs, :][0:1]
        for lo in (0, WIDTH // 2):
            sl = slice(lo, lo + WIDTH // 2)
            st_ref[sl, sl] = st[sl, sl] * dec[:, sl] + jnp.where(ind[sl, sl], kv_t[sl, sl], 0.0)
        q = hq_ref[rs, :]
        kk = k_ref[rs, :]
        bl = b_ref[rs, :]
        parts = []
        for s in range(blk):
            bs = jnp.broadcast_to(bl[s:s + 1], (blk, WIDTH))
            ks = jnp.broadcast_to(kk[s:s + 1], (blk, WIDTH))
            parts.append(q * ks * jnp.exp(jnp.where(row_i >= s, bl - bs, MASKED)))
        terms = jnp.concatenate(parts, axis=0).astype(BF16)
        a_bc = _dot(terms, ind_b)
        o_intra = jnp.zeros((blk, WIDTH), F32)
        for s in range(blk):
            o_intra = o_intra + a_bc[s * blk:(s + 1) * blk] * jnp.broadcast_to(v[s:s + 1], (blk, WIDTH))
        oacc_ref[rs, :] = o_inter + o_intra
        return carry

    lax.fori_loop(0, tt // blk, body, 0, unroll=min(HGRN_UNROLL, tt // blk))
    o = _group_rms(oacc_ref[...], og_ref[...], ind_b) * _silu(hg_ref[...])
    o_ref[...] = o.astype(o_ref.dtype)

    @pl.when(t == pl.num_programs(1) - 1)
    def _():
        s_full = st_ref[...].T
        for hd in range(HGRN_HEADS):
            sl = slice(hd * GROUP, (hd + 1) * GROUP)
            sfin_ref[hd] = s_full[sl, sl]


def _hgrn(hq, hf, hi, hg, lb_logits, out_g, state, l, batch, seq, tt):
    n = batch * seq
    nt = seq // tt
    depth = lb_logits.shape[0]
    rspec = pl.BlockSpec((tt, WIDTH), lambda b, t: (b * nt + t, 0))
    in_specs = [rspec] * 4 + [
        pl.BlockSpec((depth, WIDTH), lambda b, t: (0, 0)),
        pl.BlockSpec((None, 1, WIDTH), lambda b, t: (l, 0, 0)),
    ]
    args = [hq, hf, hi, hg, lb_logits, out_g]
    sshape = (HGRN_HEADS, GROUP, HGRN_HEAD_DIM)
    if state is not None:
        in_specs.append(pl.BlockSpec((None, None) + sshape, lambda b, t: (l, b, 0, 0, 0)))
        args.append(state)
    return pl.pallas_call(
        functools.partial(_hgrn_kernel, layer=l, depth=depth, has_state=state is not None, tt=tt),
        grid=(batch, nt),
        in_specs=in_specs,
        out_specs=[rspec, pl.BlockSpec((None,) + sshape, lambda b, t: (b, 0, 0, 0))],
        out_shape=[jax.ShapeDtypeStruct((n, WIDTH), BF16),
                   jax.ShapeDtypeStruct((batch,) + sshape, F32)],
        scratch_shapes=[pltpu.VMEM((WIDTH, WIDTH), F32)] + [pltpu.VMEM((tt, WIDTH), F32)] * 6,
        compiler_params=_params("parallel", "arbitrary"),
        name="hgrn",
    )(*args)


def _out_kernel(oa_ref, ob_ref, cu_ref, cv_ref, x_ref, vg_ref, ws_ref, bias_ref, wout_ref, g_ref,
                *out_refs, lc, emit_v):
    y_ref = out_refs[0]
    tm = x_ref.shape[0]
    ind_b = _group_indicator().astype(BF16)
    u = _gelu_tanh(cu_ref[...])
    v = _group_rms(_gelu_tanh(cv_ref[...]), vg_ref[...], ind_b)
    if emit_v:
        out_refs[1][...] = v
    vb = v.astype(BF16)
    r_i = lax.broadcasted_iota(jnp.int32, (lc, CMLP_GROUPS * lc), 0)
    c_i = lax.broadcasted_iota(jnp.int32, (lc, CMLP_GROUPS * lc), 1)
    w = jnp.where((c_i % lc) <= r_i, ws_ref[...], 0.0).astype(BF16)
    lane_g = lax.broadcasted_iota(jnp.int32, (lc, WIDTH), 1) // GROUP
    bias = bias_ref[...]
    mixed = []
    for c in range(tm // lc):
        vc = vb[c * lc:(c + 1) * lc]
        v_exp = jnp.concatenate([jnp.where(lane_g == g, vc, 0) for g in range(CMLP_GROUPS)], axis=0)
        mixed.append(_dot(w, v_exp) + bias)
    o_c = (u * jnp.concatenate(mixed, axis=0)).astype(BF16)
    na = oa_ref.shape[1]
    nb = na + WIDTH
    mix = (_dot(oa_ref[...], wout_ref[:na, :]) + _dot(ob_ref[...], wout_ref[na:nb, :])
           + _dot(o_c, wout_ref[nb:, :]))
    y_ref[...] = x_ref[...] + _rms(mix, g_ref[...])


def _out(oa, ob, cu, cv, x, norm_g, w, l, lc, emit_v, tm):
    n, d = x.shape
    rspec = lambda wd: pl.BlockSpec((tm, wd), lambda i: (i, 0))
    full = lambda a: pl.BlockSpec((None,) + a.shape[1:], lambda i: (l,) + (0,) * (a.ndim - 1))
    out_specs = [rspec(d)]
    out_shape = [jax.ShapeDtypeStruct((n, d), F32)]
    if emit_v:
        out_specs.append(rspec(WIDTH))
        out_shape.append(jax.ShapeDtypeStruct((n, WIDTH), F32))
    return pl.pallas_call(
        functools.partial(_out_kernel, lc=lc, emit_v=emit_v),
        grid=(n // tm,),
        in_specs=[rspec(oa.shape[1]), rspec(WIDTH), rspec(WIDTH), rspec(WIDTH), rspec(d),
                  full(w["v_g"]), full(w["ws"]), full(w["bias"]), full(w["wout"]),
                  pl.BlockSpec((None, None, 1, d), lambda i: (l, 3, 0, 0))],
        out_specs=out_specs,
        out_shape=out_shape,
        compiler_params=_params("parallel"),
        name="out_proj",
    )(oa, ob, cu, cv, x, w["v_g"], w["ws"], w["bias"], w["wout"], norm_g)


def _rot_cols(w):
    half = QK_ROPE // 2
    return jnp.concatenate([-w[..., half:], w[..., :half]], axis=-1)


def _pad_rope(w):
    return jnp.pad(w, [(0, 0)] * (w.ndim - 1) + [(0, ROPE_PAD - QK_ROPE)])


def _prep_weights(w_in, w_out, mla_qa_g, mla_wqb, mla_kva_g, mla_w_uk, mla_w_uv, hgrn_out_g, cmlp_v_g):
    depth = w_in.shape[0]
    o_kpe = Q_LORA + KV_LORA
    w_kpe = w_in[:, :, o_kpe:o_kpe + QK_ROPE]
    win = jnp.concatenate([w_in[:, :, :o_kpe], w_in[:, :, o_kpe + QK_ROPE:],
                           _pad_rope(w_kpe), _pad_rope(_rot_cols(w_kpe))], axis=-1).astype(BF16)
    wqb = mla_wqb.reshape(depth, Q_LORA, MLA_HEADS, QK_NOPE + QK_ROPE)
    nope = wqb[..., :QK_NOPE].reshape(depth, Q_LORA, MLA_HEADS * QK_NOPE)
    pe = wqb[..., QK_NOPE:]
    flat = lambda a: _pad_rope(a).reshape(depth, Q_LORA, MLA_HEADS * ROPE_PAD)
    wqb = jnp.concatenate([nope, flat(pe), flat(_rot_cols(pe))], axis=-1).astype(BF16)
    return dict(
        win=win, wqb=wqb,
        wuk=jnp.transpose(mla_w_uk, (0, 2, 3, 1)).astype(BF16),
        wuv=jnp.transpose(mla_w_uv, (0, 2, 1, 3)).astype(BF16),
        wout=w_out.astype(BF16),
        qa_g=mla_qa_g[:, None, :], kva_g=mla_kva_g[:, None, :],
        out_g=hgrn_out_g[:, None, :], v_g=cmlp_v_g[:, None, :],
    )


def _rope_tables(pos, rows):
    inv_freq = ROPE_BASE ** (-jnp.arange(0, QK_ROPE, 2, dtype=F32) / QK_ROPE)
    ang = pos.astype(F32)[:, None] * inv_freq[None, :]
    pad = jnp.zeros((pos.shape[0], ROPE_PAD - QK_ROPE), F32)
    cos = jnp.concatenate([jnp.cos(ang), jnp.cos(ang), pad], axis=-1)
    sin = jnp.concatenate([jnp.sin(ang), jnp.sin(ang), pad], axis=-1)
    reps = max(1, rows // pos.shape[0])
    return jnp.tile(cos, (reps, 1)), jnp.tile(sin, (reps, 1))


def _tile(n, want):
    return want if n % want == 0 else n


def _trunk(x3, pos, cache_ckv, cache_kpe, state, norm_g, ffn_w, w, lb_logits, cmlp_w_s, cmlp_b_s):
    batch, seq, d = x3.shape
    n = batch * seq
    depth = norm_g.shape[0]
    x = x3.reshape(n, d)
    tm = _tile(n, 512)
    cos, sin = _rope_tables(pos, tm)
    lc = min(CMLP_CHUNK, seq)
    ws = jnp.transpose(cmlp_w_s[:, :, :lc, :lc], (0, 2, 1, 3)).reshape(depth, lc, CMLP_GROUPS * lc)
    bias = jnp.repeat(jnp.transpose(cmlp_b_s[:, :, :lc], (0, 2, 1)), GROUP, axis=-1)
    w = dict(w, ws=ws, bias=bias)
    is_sample = cache_ckv is not None
    ckv_rows, kpe_rows, states, v_rows = [], [], [], []
    for l in range(depth):
        x = _ffn(x, norm_g, *ffn_w, l, 0, tm)
        tq = None if is_sample else _tile(seq, 256)
        q, c_new, kpe_new, kcat, hq, hf, hi, hg, cu, cv, *ct = _inproj(x, norm_g, l, w, cos, sin, tm, tq)
        if is_sample:
            o_a = _attn_sample(q, cache_ckv, cache_kpe, kcat, w["wuv"], l, batch, seq)
        else:
            o_a = _attn_prompt(q, kcat, ct[0], w["wuv"], l, batch, seq, tq)
        o_b, s_new = _hgrn(hq, hf, hi, hg, lb_logits, w["out_g"], state, l, batch, seq,
                           _tile(seq, 256))
        outs = _out(o_a, o_b, cu, cv, x, norm_g, w, l, lc, is_sample, tm)
        x = outs[0]
        x = _ffn(x, norm_g, *ffn_w, l, 1, tm)
        ckv_rows.append(c_new.reshape(batch, seq, KV_LORA))
        kpe_rows.append(kpe_new.reshape(batch, seq, QK_ROPE))
        states.append(s_new)
        if is_sample:
            v_rows.append(outs[1].reshape(batch, seq, WIDTH))
    res = [x.reshape(batch, seq, d), jnp.stack(ckv_rows), jnp.stack(kpe_rows), jnp.stack(states)]
    if is_sample:
        res.append(jnp.stack(v_rows))
    return res


def kernel(x_prompt, x_sample, cache_mla_ckv, cache_mla_kpe, state_hgrn, norm_g, ffn_w_gate, ffn_w_up,
           ffn_w_down, w_in, w_out, mla_qa_g, mla_wqb, mla_kva_g, mla_w_uk, mla_w_uv, hgrn_lb_logits,
           hgrn_out_g, cmlp_v_g, cmlp_w_s, cmlp_b_s):
    w = _prep_weights(w_in, w_out, mla_qa_g, mla_wqb, mla_kva_g, mla_w_uk, mla_w_uv, hgrn_out_g,
                      cmlp_v_g)
    ffn_w = (ffn_w_gate.astype(BF16), ffn_w_up.astype(BF16), ffn_w_down.astype(BF16))
    depth = norm_g.shape[0]
    ng = norm_g.reshape(depth, norm_g.shape[1], 1, norm_g.shape[2])
    common = (ng, ffn_w, w, hgrn_lb_logits, cmlp_w_s, cmlp_b_s)
    pos_p = jnp.arange(x_prompt.shape[1], dtype=jnp.int32)
    y_p, ckv_p, kpe_p, hgrn_p = _trunk(x_prompt, pos_p, None, None, None, *common)
    pos_s = cache_mla_ckv.shape[2] + jnp.arange(x_sample.shape[1], dtype=jnp.int32)
    y_s, ckv_s, kpe_s, hgrn_s, v_s = _trunk(x_sample, pos_s, cache_mla_ckv, cache_mla_kpe, state_hgrn,
                                            *common)
    return (y_p, y_s, ckv_p, kpe_p, hgrn_p, ckv_s, kpe_s, hgrn_s, v_s)
```

```python
import functools

import jax
import jax.numpy as jnp
from jax import lax
from jax.experimental import pallas as pl
from jax.experimental.pallas import tpu as pltpu

F32 = jnp.float32
BF16 = jnp.bfloat16

EPS = 1e-6
CHUNK = 64
MLA_HEADS = 4
Q_LORA = 384
KV_LORA = 256
QK_NOPE = 128
QK_ROPE = 64
V_HEAD = 128
MLA_SCALE = (QK_NOPE + QK_ROPE) ** -0.5
ROPE_BASE = 10000.0
HGRN_HEADS = 4
HGRN_HEAD_DIM = 64
GROUP = 64
WIDTH = 256
CMLP_GROUPS = 4
CMLP_CHUNK = 128
LANES = 128
ROPE_PAD = LANES
QK_CAT = KV_LORA + ROPE_PAD
Q_SCALE = MLA_SCALE * 1.4426950408889634
HGRN_BLOCK = 16
HGRN_UNROLL = 16
MASKED = -1e30

VMEM_LIMIT = 56 * 1024 * 1024


def _params(*sem):
    return pltpu.CompilerParams(dimension_semantics=sem, vmem_limit_bytes=VMEM_LIMIT)


def _dot(a, b):
    return jnp.dot(a, b, preferred_element_type=F32)


def _dot_nt(a, b):
    return lax.dot_general(a, b, (((1,), (1,)), ((), ())), preferred_element_type=F32)


def _dot_tn(a, b):
    return lax.dot_general(a, b, (((0,), (0,)), ((), ())), preferred_element_type=F32)


def _rms(x, g):
    return x * lax.rsqrt(jnp.mean(x * x, axis=-1, keepdims=True) + EPS) * g


def _silu(x):
    return x * jax.nn.sigmoid(x)


def _gelu_tanh(x):
    return 0.5 * x * (1.0 + jnp.tanh(0.7978845608028654 * (x + 0.044715 * (x * x * x))))


def _group_indicator():
    r = lax.broadcasted_iota(jnp.int32, (WIDTH, WIDTH), 0) // GROUP
    c = lax.broadcasted_iota(jnp.int32, (WIDTH, WIDTH), 1) // GROUP
    return r == c


def _group_rms(x, g, ind_b):
    ms = _dot((x * x).astype(BF16), ind_b) * (1.0 / GROUP)
    return x * lax.rsqrt(ms + EPS) * g


def _split3(x):
    hi = x.astype(BF16)
    r1 = x - hi.astype(F32)
    mid = r1.astype(BF16)
    lo = (r1 - mid.astype(F32)).astype(BF16)
    return hi, mid, lo


def _ffn_kernel(x_ref, gpre_ref, gpost_ref, wg_ref, wu_ref, wd_ref, o_ref, *, tf):
    x = x_ref[...]
    h = _rms(x, gpre_ref[...]).astype(BF16)
    d_ff = wg_ref.shape[1]
    acc = jnp.zeros(x.shape, F32)
    for c in range(d_ff // tf):
        sl = slice(c * tf, (c + 1) * tf)
        g = _dot(h, wg_ref[:, sl])
        u = _dot(h, wu_ref[:, sl])
        a = (_silu(g) * u).astype(BF16)
        acc = acc + _dot(a, wd_ref[sl, :])
    o_ref[...] = x + 0.5 * _rms(acc, gpost_ref[...])


def _ffn(x, norm_g, wg, wu, wd, l, j, tm):
    n, d = x.shape
    d_ff = wg.shape[-1]
    const = lambda r, c: pl.BlockSpec((None, None, r, c), lambda i: (l, j, 0, 0))
    gain = lambda k: pl.BlockSpec((None, None, 1, d), lambda i: (l, k, 0, 0))
    return pl.pallas_call(
        functools.partial(_ffn_kernel, tf=256),
        grid=(n // tm,),
        in_specs=[pl.BlockSpec((tm, d), lambda i: (i, 0)), gain(4 * j), gain(4 * j + 1),
                  const(d, d_ff), const(d, d_ff), const(d_ff, d)],
        out_specs=pl.BlockSpec((tm, d), lambda i: (i, 0)),
        out_shape=jax.ShapeDtypeStruct((n, d), F32),
        compiler_params=_params("parallel"),
        name="ffn",
    )(x, norm_g, norm_g, wg, wu, wd)


def _inproj_kernel(x_ref, g_ref, win_ref, qag_ref, kvag_ref, wqb_ref, wuk_ref, cos_ref, sin_ref,
                   q_ref, c_ref, kpe_ref, kcat_ref, hq_ref, hf_ref, hi_ref, hg_ref, cu_ref, cv_ref,
                   ct_ref=None):
    h = _rms(x_ref[...], g_ref[...]).astype(BF16)
    z = _dot(h, win_ref[...])
    cos = cos_ref[...]
    sin = sin_ref[...]
    o = Q_LORA
    c_new = _rms(z[:, o:o + KV_LORA], kvag_ref[...])
    c_ref[...] = c_new
    kcat_ref[:, :KV_LORA] = c_new.astype(BF16)
    if ct_ref is not None:
        tk = ct_ref.shape[2]
        for u in range(ct_ref.shape[0]):
            ct_ref[u] = c_new[u * tk:(u + 1) * tk].T.astype(BF16)
    o += KV_LORA
    for ref in (hq_ref, hf_ref, hi_ref, hg_ref, cu_ref, cv_ref):
        ref[...] = z[:, o:o + WIDTH]
        o += WIDTH
    kpe = z[:, o:o + ROPE_PAD] * cos + z[:, o + ROPE_PAD:o + 2 * ROPE_PAD] * sin
    kpe_ref[...] = kpe[:, :QK_ROPE]
    kcat_ref[:, KV_LORA:] = kpe.astype(BF16)
    qn = _rms(z[:, :Q_LORA], qag_ref[...]).astype(BF16)
    qq = _dot(qn, wqb_ref[...])
    pe0 = MLA_HEADS * QK_NOPE
    per0 = pe0 + MLA_HEADS * ROPE_PAD
    for hd in range(MLA_HEADS):
        pe = qq[:, pe0 + hd * ROPE_PAD:pe0 + (hd + 1) * ROPE_PAD]
        per = qq[:, per0 + hd * ROPE_PAD:per0 + (hd + 1) * ROPE_PAD]
        q_pe = (pe * cos + per * sin) * Q_SCALE
        nope = qq[:, hd * QK_NOPE:(hd + 1) * QK_NOPE].astype(BF16)
        q_lat = _dot(nope, wuk_ref[hd]) * Q_SCALE
        if ct_ref is not None:
            q_ref[hd, :KV_LORA, :] = q_lat.T.astype(BF16)
            q_ref[hd, KV_LORA:, :] = q_pe.T.astype(BF16)
        else:
            q_ref[hd, :, :KV_LORA] = q_lat.astype(BF16)
            q_ref[hd, :, KV_LORA:] = q_pe.astype(BF16)


def _inproj(x, norm_g, l, w, cos, sin, tm, tk):
    n, d = x.shape
    nt = cos.shape[0] // tm
    row = lambda wd, dt: jax.ShapeDtypeStruct((n, wd), dt)
    rspec = lambda wd: pl.BlockSpec((tm, wd), lambda i: (i, 0))
    full = lambda a: pl.BlockSpec((None,) + a.shape[1:], lambda i: (l,) + (0,) * (a.ndim - 1))
    out_specs = [pl.BlockSpec((MLA_HEADS, tm, QK_CAT), lambda i: (0, i, 0)),
                 rspec(KV_LORA), rspec(QK_ROPE), rspec(QK_CAT)] + [rspec(WIDTH)] * 6
    out_shape = [jax.ShapeDtypeStruct((MLA_HEADS, n, QK_CAT), BF16),
                 row(KV_LORA, F32), row(QK_ROPE, F32), row(QK_CAT, BF16)] + [row(WIDTH, F32)] * 6
    if tk is not None:
        out_specs[0] = pl.BlockSpec((MLA_HEADS, QK_CAT, tm), lambda i: (0, 0, i))
        out_shape[0] = jax.ShapeDtypeStruct((MLA_HEADS, QK_CAT, n), BF16)
        out_specs.append(pl.BlockSpec((tm // tk, KV_LORA, tk), lambda i: (i, 0, 0)))
        out_shape.append(jax.ShapeDtypeStruct((n // tk, KV_LORA, tk), BF16))
    return pl.pallas_call(
        _inproj_kernel,
        grid=(n // tm,),
        in_specs=[
            rspec(d),
            pl.BlockSpec((None, None, 1, d), lambda i: (l, 2, 0, 0)),
            full(w["win"]), full(w["qa_g"]), full(w["kva_g"]), full(w["wqb"]), full(w["wuk"]),
            pl.BlockSpec((tm, ROPE_PAD), lambda i: (i % nt, 0)),
            pl.BlockSpec((tm, ROPE_PAD), lambda i: (i % nt, 0)),
        ],
        out_specs=out_specs,
        out_shape=out_shape,
        compiler_params=_params("parallel"),
        name="in_proj",
    )(x, norm_g, w["win"], w["qa_g"], w["kva_g"], w["wqb"], w["wuk"], cos, sin)


def _attn_prompt_kernel(q_ref, kcat_ref, ct_ref, wuv_ref, o_ref, m_ref, l_ref, acc_ref, *, tq):
    i = pl.program_id(1)
    heads = range(MLA_HEADS)
    m_ref[...] = jnp.full(m_ref.shape, -jnp.inf, F32)
    l_ref[...] = jnp.zeros(l_ref.shape, F32)
    acc_ref[...] = jnp.zeros(acc_ref.shape, F32)

    def scores(j, hd):
        kcat = kcat_ref[pl.ds(pl.multiple_of(j * tq, tq), tq), :]
        return _dot(kcat, q_ref[hd])

    def softmax_pv(s, j, hd, diagonal):
        if diagonal:
            kch = lax.broadcasted_iota(jnp.int32, (tq, tq), 0) // CHUNK
            qc = lax.broadcasted_iota(jnp.int32, (tq, tq), 1) // CHUNK
            s = jnp.where(kch <= qc, s, -jnp.inf)
        m_prev = m_ref[hd]
        m_new = jnp.maximum(m_prev, jnp.max(s, axis=0, keepdims=True))
        alpha = jnp.exp2(m_prev - m_new)
        p = jnp.exp2(s - m_new)
        l_ref[hd] = alpha * l_ref[hd] + jnp.sum(p, axis=0, keepdims=True)
        acc_ref[hd] = alpha * acc_ref[hd] + _dot(ct_ref[j], p.astype(BF16))
        m_ref[hd] = m_new

    s0 = tuple(scores(0, hd) for hd in heads)

    def body(j, s_cur):
        s_next = []
        for hd in heads:
            s_next.append(scores(j + 1, hd))
            softmax_pv(s_cur[hd], j, hd, False)
        return tuple(s_next)

    def trips(per_trip, first):
        def trip(t, s):
            for u in range(per_trip):
                s = body(first + per_trip * t + u, s)
            return s
        return trip

    r1 = i % 2
    r2 = (i // 2) % 2
    s_last = lax.fori_loop(0, r1, trips(1, 0), s0)
    s_last = lax.fori_loop(0, r2, trips(2, r1), s_last)
    r4 = (i // 4) % 2
    s_last = lax.fori_loop(0, r4, trips(4, r1 + 2 * r2), s_last)
    s_last = lax.fori_loop(0, i // 8, trips(8, r1 + 2 * r2 + 4 * r4), s_last)
    for hd in heads:
        softmax_pv(s_last[hd], i, hd, True)
    for hd in heads:
        o_lat_t = (acc_ref[hd] / l_ref[hd]).astype(BF16)
        o_ref[:, hd * V_HEAD:(hd + 1) * V_HEAD] = _dot_tn(o_lat_t, wuv_ref[hd]).astype(o_ref.dtype)


def _attn_prompt(q, kcat, ct, wuv, l, batch, seq, tq):
    n = batch * seq
    nq = seq // tq
    return pl.pallas_call(
        functools.partial(_attn_prompt_kernel, tq=tq),
        grid=(batch, nq),
        in_specs=[
            pl.BlockSpec((MLA_HEADS, QK_CAT, tq), lambda b, i: (0, 0, b * nq + i)),
            pl.BlockSpec((seq, QK_CAT), lambda b, i: (b, 0)),
            pl.BlockSpec((nq, KV_LORA, tq), lambda b, i: (b, 0, 0)),
            pl.BlockSpec((None,) + wuv.shape[1:], lambda b, i: (l, 0, 0, 0)),
        ],
        out_specs=pl.BlockSpec((tq, MLA_HEADS * V_HEAD), lambda b, i: (b * nq + i, 0)),
        out_shape=jax.ShapeDtypeStruct((n, MLA_HEADS * V_HEAD), BF16),
        scratch_shapes=[pltpu.VMEM((MLA_HEADS, 1, tq), F32), pltpu.VMEM((MLA_HEADS, 1, tq), F32),
                        pltpu.VMEM((MLA_HEADS, KV_LORA, tq), F32)],
        compiler_params=_params("parallel", "arbitrary"),
        name="attn_prompt",
    )(q, kcat, ct, wuv)


def _attn_sample_kernel(q_ref, ckv_ref, ckpe_ref, kcat_ref, wuv_ref, o_ref, *, tq, nb):
    rows = MLA_HEADS * tq

    def scores(u):
        q = q_ref[:, u * tq:(u + 1) * tq, :].reshape(rows, QK_CAT)
        c_past = ckv_ref[u].astype(BF16)
        p_past = ckpe_ref[u].astype(BF16)
        kcat = kcat_ref[u * tq:(u + 1) * tq, :]
        s_past = _dot_nt(q[:, :KV_LORA], c_past) + _dot_nt(q[:, KV_LORA:KV_LORA + QK_ROPE], p_past)
        return s_past, _dot_nt(q, kcat), c_past, kcat[:, :KV_LORA]

    def finish(u, s_past, s_new, c_past, kc):
        m = jnp.maximum(jnp.max(s_past, axis=-1, keepdims=True), jnp.max(s_new, axis=-1, keepdims=True))
        e_past = jnp.exp2(s_past - m)
        e_new = jnp.exp2(s_new - m)
        l = jnp.sum(e_past, axis=-1, keepdims=True) + jnp.sum(e_new, axis=-1, keepdims=True)
        acc = _dot(e_past.astype(BF16), c_past) + _dot(e_new.astype(BF16), kc)
        o_lat = (acc / l).astype(BF16)
        for hd in range(MLA_HEADS):
            o_ref[u * tq:(u + 1) * tq, hd * V_HEAD:(hd + 1) * V_HEAD] = _dot(
                o_lat[hd * tq:(hd + 1) * tq], wuv_ref[hd]).astype(o_ref.dtype)

    staged = [scores(u) for u in range(nb)]
    for u in range(nb):
        finish(u, *staged[u])


def _attn_sample(q, cache_ckv, cache_kpe, kcat, wuv, l, batch, seq):
    n = batch * seq
    past = cache_ckv.shape[2]
    nb = max(k for k in (1, 2, 4) if batch % k == 0)
    return pl.pallas_call(
        functools.partial(_attn_sample_kernel, tq=seq, nb=nb),
        grid=(batch // nb,),
        in_specs=[
            pl.BlockSpec((MLA_HEADS, nb * seq, QK_CAT), lambda b: (0, b, 0)),
            pl.BlockSpec((None, nb, past, KV_LORA), lambda b: (l, b, 0, 0)),
            pl.BlockSpec((None, nb, past, QK_ROPE), lambda b: (l, b, 0, 0)),
            pl.BlockSpec((nb * seq, QK_CAT), lambda b: (b, 0)),
            pl.BlockSpec((None,) + wuv.shape[1:], lambda b: (l, 0, 0, 0)),
        ],
        out_specs=pl.BlockSpec((nb * seq, MLA_HEADS * V_HEAD), lambda b: (b, 0)),
        out_shape=jax.ShapeDtypeStruct((n, MLA_HEADS * V_HEAD), BF16),
        compiler_params=_params("parallel"),
        name="attn_sample",
    )(q, cache_ckv, cache_kpe, kcat, wuv)


def _hgrn_lower_bound(lg, layer, depth):
    rows = [lg[r:r + 1] for r in range(depth)]
    mx = functools.reduce(jnp.maximum, rows)
    ex = [jnp.exp(r - mx) for r in rows]
    tot = functools.reduce(lambda a, b: a + b, ex)
    sm = [e / tot for e in ex]
    cum0 = sm[0]
    cum = functools.reduce(lambda a, b: a + b, sm[:layer + 1])
    return cum - cum0


def _hgrn_kernel(*refs, layer, depth, has_state, tt):
    hq_ref, hf_ref, hi_ref, hg_ref, lbl_ref, og_ref = refs[:6]
    refs = refs[6:]
    if has_state:
        s0_ref, refs = refs[0], refs[1:]
    o_ref, sfin_ref, st_ref, qd_ref, kd_ref, dec_ref, k_ref, b_ref, oacc_ref = refs
    t = pl.program_id(1)
    blk = HGRN_BLOCK
    ind = _group_indicator()
    ind_b = ind.astype(BF16)

    @pl.when(t == 0)
    def _():
        if has_state:
            s0 = s0_ref[...].reshape(WIDTH, HGRN_HEAD_DIM)
            tiled = jnp.concatenate([s0] * HGRN_HEADS, axis=1)
            st_ref[...] = jnp.where(ind, tiled, 0.0).T
        else:
            st_ref[...] = jnp.zeros((WIDTH, WIDTH), F32)

    lb = _hgrn_lower_bound(lbl_ref[...], layer, depth)
    zf = hf_ref[...]
    log_sig = jnp.minimum(zf, 0.0) - jnp.log1p(jnp.exp(-jnp.abs(zf)))
    a = jnp.log(lb)
    b = jnp.log1p(-lb) + log_sig
    logf = jnp.maximum(a, b) + jnp.log1p(jnp.exp(-jnp.abs(a - b)))
    k = (1.0 - lb) * jax.nn.sigmoid(-zf)

    r_i = lax.broadcasted_iota(jnp.int32, (tt, tt), 0)
    c_i = lax.broadcasted_iota(jnp.int32, (tt, tt), 1)
    same = (r_i // blk) == (c_i // blk)
    tri = (same & (c_i <= r_i)).astype(BF16)
    ones = same.astype(BF16)
    hi, mid, lo = _split3(logf)
    bloc = _dot(tri, hi) + _dot(tri, mid) + _dot(tri, lo)
    btot = _dot(ones, hi) + _dot(ones, mid) + _dot(ones, lo)
    qd_ref[...] = hq_ref[...] * jnp.exp(bloc)
    kd_ref[...] = k * jnp.exp(btot - bloc)
    dec_ref[...] = jnp.exp(btot)
    k_ref[...] = k
    b_ref[...] = bloc

    row_i = lax.broadcasted_iota(jnp.int32, (blk, WIDTH), 0)

    def body(i, carry):
        rs = pl.ds(pl.multiple_of(i * blk, blk), blk)
        st = st_ref[...]
        v = hi_ref[rs, :]
        o_inter = _dot_nt(qd_ref[rs, :].astype(BF16), st.astype(BF16))
        kv_t = _dot_tn(v.astype(BF16), kd_ref[rs, :].astype(BF16))
        dec = dec_ref[rs, :][0:1]
        for lo in (0, WIDTH // 2):
            sl = slice(lo, lo + WIDTH // 2)
            st_ref[sl, sl] = st[sl, sl] * dec[:, sl] + jnp.where(ind[sl, sl], kv_t[sl, sl], 0.0)
        q = hq_ref[rs, :]
        kk = k_ref[rs, :]
        bl = b_ref[rs, :]
        parts = []
        for s in range(blk):
            bs = jnp.broadcast_to(bl[s:s + 1], (blk, WIDTH))
            ks = jnp.broadcast_to(kk[s:s + 1], (blk, WIDTH))
            parts.append(q * ks * jnp.exp(jnp.where(row_i >= s, bl - bs, MASKED)))
        terms = jnp.concatenate(parts, axis=0).astype(BF16)
        a_bc = _dot(terms, ind_b)
        o_intra = jnp.zeros((blk, WIDTH), F32)
        for s in range(blk):
            o_intra = o_intra + a_bc[s * blk:(s + 1) * blk] * jnp.broadcast_to(v[s:s + 1], (blk, WIDTH))
        oacc_ref[rs, :] = o_inter + o_intra
        return carry

    lax.fori_loop(0, tt // blk, body, 0, unroll=min(HGRN_UNROLL, tt // blk))
    o = _group_rms(oacc_ref[...], og_ref[...], ind_b) * _silu(hg_ref[...])
    o_ref[...] = o.astype(o_ref.dtype)

    @pl.when(t == pl.num_programs(1) - 1)
    def _():
        s_full = st_ref[...].T
        for hd in range(HGRN_HEADS):
            sl = slice(hd * GROUP, (hd + 1) * GROUP)
            sfin_ref[hd] = s_full[sl, sl]


def _hgrn(hq, hf, hi, hg, lb_logits, out_g, state, l, batch, seq, tt):
    n = batch * seq
    nt = seq // tt
    depth = lb_logits.shape[0]
    rspec = pl.BlockSpec((tt, WIDTH), lambda b, t: (b * nt + t, 0))
    in_specs = [rspec] * 4 + [
        pl.BlockSpec((depth, WIDTH), lambda b, t: (0, 0)),
        pl.BlockSpec((None, 1, WIDTH), lambda b, t: (l, 0, 0)),
    ]
    args = [hq, hf, hi, hg, lb_logits, out_g]
    sshape = (HGRN_HEADS, GROUP, HGRN_HEAD_DIM)
    if state is not None:
        in_specs.append(pl.BlockSpec((None, None) + sshape, lambda b, t: (l, b, 0, 0, 0)))
        args.append(state)
    return pl.pallas_call(
        functools.partial(_hgrn_kernel, layer=l, depth=depth, has_state=state is not None, tt=tt),
        grid=(batch, nt),
        in_specs=in_specs,
        out_specs=[rspec, pl.BlockSpec((None,) + sshape, lambda b, t: (b, 0, 0, 0))],
        out_shape=[jax.ShapeDtypeStruct((n, WIDTH), BF16),
                   jax.ShapeDtypeStruct((batch,) + sshape, F32)],
        scratch_shapes=[pltpu.VMEM((WIDTH, WIDTH), F32)] + [pltpu.VMEM((tt, WIDTH), F32)] * 6,
        compiler_params=_params("parallel", "arbitrary"),
        name="hgrn",
    )(*args)


def _out_kernel(oa_ref, ob_ref, cu_ref, cv_ref, x_ref, vg_ref, ws_ref, bias_ref, wout_ref, g_ref,
                *out_refs, lc, emit_v):
    y_ref = out_refs[0]
    tm = x_ref.shape[0]
    ind_b = _group_indicator().astype(BF16)
    u = _gelu_tanh(cu_ref[...])
    v = _group_rms(_gelu_tanh(cv_ref[...]), vg_ref[...], ind_b)
    if emit_v:
        out_refs[1][...] = v
    vb = v.astype(BF16)
    r_i = lax.broadcasted_iota(jnp.int32, (lc, CMLP_GROUPS * lc), 0)
    c_i = lax.broadcasted_iota(jnp.int32, (lc, CMLP_GROUPS * lc), 1)
    w = jnp.where((c_i % lc) <= r_i, ws_ref[...], 0.0).astype(BF16)
    lane_g = lax.broadcasted_iota(jnp.int32, (lc, WIDTH), 1) // GROUP
    bias = bias_ref[...]
    mixed = []
    for c in range(tm // lc):
        vc = vb[c * lc:(c + 1) * lc]
        v_exp = jnp.concatenate([jnp.where(lane_g == g, vc, 0) for g in range(CMLP_GROUPS)], axis=0)
        mixed.append(_dot(w, v_exp) + bias)
    o_c = (u * jnp.concatenate(mixed, axis=0)).astype(BF16)
    na = oa_ref.shape[1]
    nb = na + WIDTH
    mix = (_dot(oa_ref[...], wout_ref[:na, :]) + _dot(ob_ref[...], wout_ref[na:nb, :])
           + _dot(o_c, wout_ref[nb:, :]))
    y_ref[...] = x_ref[...] + _rms(mix, g_ref[...])


def _out(oa, ob, cu, cv, x, norm_g, w, l, lc, emit_v, tm):
    n, d = x.shape
    rspec = lambda wd: pl.BlockSpec((tm, wd), lambda i: (i, 0))
    full = lambda a: pl.BlockSpec((None,) + a.shape[1:], lambda i: (l,) + (0,) * (a.ndim - 1))
    out_specs = [rspec(d)]
    out_shape = [jax.ShapeDtypeStruct((n, d), F32)]
    if emit_v:
        out_specs.append(rspec(WIDTH))
        out_shape.append(jax.ShapeDtypeStruct((n, WIDTH), F32))
    return pl.pallas_call(
        functools.partial(_out_kernel, lc=lc, emit_v=emit_v),
        grid=(n // tm,),
        in_specs=[rspec(oa.shape[1]), rspec(WIDTH), rspec(WIDTH), rspec(WIDTH), rspec(d),
                  full(w["v_g"]), full(w["ws"]), full(w["bias"]), full(w["wout"]),
                  pl.BlockSpec((None, None, 1, d), lambda i: (l, 3, 0, 0))],
        out_specs=out_specs,
        out_shape=out_shape,
        compiler_params=_params("parallel"),
        name="out_proj",
    )(oa, ob, cu, cv, x, w["v_g"], w["ws"], w["bias"], w["wout"], norm_g)


def _rot_cols(w):
    half = QK_ROPE // 2
    return jnp.concatenate([-w[..., half:], w[..., :half]], axis=-1)


def _pad_rope(w):
    return jnp.pad(w, [(0, 0)] * (w.ndim - 1) + [(0, ROPE_PAD - QK_ROPE)])


def _prep_weights(w_in, w_out, mla_qa_g, mla_wqb, mla_kva_g, mla_w_uk, mla_w_uv, hgrn_out_g, cmlp_v_g):
    depth = w_in.shape[0]
    o_kpe = Q_LORA + KV_LORA
    w_kpe = w_in[:, :, o_kpe:o_kpe + QK_ROPE]
    win = jnp.concatenate([w_in[:, :, :o_kpe], w_in[:, :, o_kpe + QK_ROPE:],
                           _pad_rope(w_kpe), _pad_rope(_rot_cols(w_kpe))], axis=-1).astype(BF16)
    wqb = mla_wqb.reshape(depth, Q_LORA, MLA_HEADS, QK_NOPE + QK_ROPE)
    nope = wqb[..., :QK_NOPE].reshape(depth, Q_LORA, MLA_HEADS * QK_NOPE)
    pe = wqb[..., QK_NOPE:]
    flat = lambda a: _pad_rope(a).reshape(depth, Q_LORA, MLA_HEADS * ROPE_PAD)
    wqb = jnp.concatenate([nope, flat(pe), flat(_rot_cols(pe))], axis=-1).astype(BF16)
    return dict(
        win=win, wqb=wqb,
        wuk=jnp.transpose(mla_w_uk, (0, 2, 3, 1)).astype(BF16),
        wuv=jnp.transpose(mla_w_uv, (0, 2, 1, 3)).astype(BF16),
        wout=w_out.astype(BF16),
        qa_g=mla_qa_g[:, None, :], kva_g=mla_kva_g[:, None, :],
        out_g=hgrn_out_g[:, None, :], v_g=cmlp_v_g[:, None, :],
    )


def _rope_tables(pos, rows):
    inv_freq = ROPE_BASE ** (-jnp.arange(0, QK_ROPE, 2, dtype=F32) / QK_ROPE)
    ang = pos.astype(F32)[:, None] * inv_freq[None, :]
    pad = jnp.zeros((pos.shape[0], ROPE_PAD - QK_ROPE), F32)
    cos = jnp.concatenate([jnp.cos(ang), jnp.cos(ang), pad], axis=-1)
    sin = jnp.concatenate([jnp.sin(ang), jnp.sin(ang), pad], axis=-1)
    reps = max(1, rows // pos.shape[0])
    return jnp.tile(cos, (reps, 1)), jnp.tile(sin, (reps, 1))


def _tile(n, want):
    return want if n % want == 0 else n


def _trunk(x3, pos, cache_ckv, cache_kpe, state, norm_g, ffn_w, w, lb_logits, cmlp_w_s, cmlp_b_s):
    batch, seq, d = x3.shape
    n = batch * seq
    depth = norm_g.shape[0]
    x = x3.reshape(n, d)
    tm = _tile(n, 512)
    cos, sin = _rope_tables(pos, tm)
    lc = min(CMLP_CHUNK, seq)
    ws = jnp.transpose(cmlp_w_s[:, :, :lc, :lc], (0, 2, 1, 3)).reshape(depth, lc, CMLP_GROUPS * lc)
    bias = jnp.repeat(jnp.transpose(cmlp_b_s[:, :, :lc], (0, 2, 1)), GROUP, axis=-1)
    w = dict(w, ws=ws, bias=bias)
    is_sample = cache_ckv is not None
    ckv_rows, kpe_rows, states, v_rows = [], [], [], []
    for l in range(depth):
        x = _ffn(x, norm_g, *ffn_w, l, 0, tm)
        tq = None if is_sample else _tile(seq, 256)
        q, c_new, kpe_new, kcat, hq, hf, hi, hg, cu, cv, *ct = _inproj(x, norm_g, l, w, cos, sin, tm, tq)
        if is_sample:
            o_a = _attn_sample(q, cache_ckv, cache_kpe, kcat, w["wuv"], l, batch, seq)
        else:
            o_a = _attn_prompt(q, kcat, ct[0], w["wuv"], l, batch, seq, tq)
        o_b, s_new = _hgrn(hq, hf, hi, hg, lb_logits, w["out_g"], state, l, batch, seq,
                           _tile(seq, 256))
        outs = _out(o_a, o_b, cu, cv, x, norm_g, w, l, lc, is_sample, tm)
        x = outs[0]
        x = _ffn(x, norm_g, *ffn_w, l, 1, tm)
        ckv_rows.append(c_new.reshape(batch, seq, KV_LORA))
        kpe_rows.append(kpe_new.reshape(batch, seq, QK_ROPE))
        states.append(s_new)
        if is_sample:
            v_rows.append(outs[1].reshape(batch, seq, WIDTH))
    res = [x.reshape(batch, seq, d), jnp.stack(ckv_rows), jnp.stack(kpe_rows), jnp.stack(states)]
    if is_sample:
        res.append(jnp.stack(v_rows))
    return res


def kernel(x_prompt, x_sample, cache_mla_ckv, cache_mla_kpe, state_hgrn, norm_g, ffn_w_gate, ffn_w_up,
           ffn_w_down, w_in, w_out, mla_qa_g, mla_wqb, mla_kva_g, mla_w_uk, mla_w_uv, hgrn_lb_logits,
           hgrn_out_g, cmlp_v_g, cmlp_w_s, cmlp_b_s):
    w = _prep_weights(w_in, w_out, mla_qa_g, mla_wqb, mla_kva_g, mla_w_uk, mla_w_uv, hgrn_out_g,
                      cmlp_v_g)
    ffn_w = (ffn_w_gate.astype(BF16), ffn_w_up.astype(BF16), ffn_w_down.astype(BF16))
    depth = norm_g.shape[0]
    ng = norm_g.reshape(depth, norm_g.shape[1], 1, norm_g.shape[2])
    common = (ng, ffn_w, w, hgrn_lb_logits, cmlp_w_s, cmlp_b_s)
    pos_p = jnp.arange(x_prompt.shape[1], dtype=jnp.int32)
    y_p, ckv_p, kpe_p, hgrn_p = _trunk(x_prompt, pos_p, None, None, None, *common)
    pos_s = cache_mla_ckv.shape[2] + jnp.arange(x_sample.shape[1], dtype=jnp.int32)
    y_s, ckv_s, kpe_s, hgrn_s, v_s = _trunk(x_sample, pos_s, cache_mla_ckv, cache_mla_kpe, state_hgrn,
                                            *common)
    return (y_p, y_s, ckv_p, kpe_p, hgrn_p, ckv_s, kpe_s, hgrn_s, v_s)
```
